```python
import math, functools
import jax, jax.numpy as jnp
from jax import lax
import numpy as np

D_MODEL = 1024
BATCH = 8
SEQ = 2048
DEPTH = 4
DEC_BATCH = 8
DEC_SEQ = 16
PAST_LEN = 2048

CHUNK = 64
Q_BLOCK = 128
EPS = 1e-6
ROPE_THETA = 10000.0
NEG = -1e30

A_HEADS = 8
A_KV_HEADS = 2
A_HEAD_DIM = 64
IDX_HEADS = 8
IDX_DIM = 64
TOPK_MAX = 256
B_HEADS = 8
B_HEAD_DIM = 64
P_HEADS = 8
N_KEYS = 128
N_EXPERTS = N_KEYS * N_KEYS
P_KEY_DIM = 128
P_HALF = P_KEY_DIM // 2
P_TOPK = 16
PEER_BLOCK = 128

W_QA = A_HEADS * A_HEAD_DIM
W_KA = A_KV_HEADS * A_HEAD_DIM
W_VA = A_KV_HEADS * A_HEAD_DIM
W_QI = IDX_HEADS * IDX_DIM
W_KI = IDX_DIM
W_WI = IDX_HEADS
W_QB = B_HEADS * B_HEAD_DIM
W_KB = B_HEADS * B_HEAD_DIM
W_VB = B_HEADS * B_HEAD_DIM
W_G = D_MODEL
SPLITS = (W_QA, W_KA, W_VA, W_QI, W_KI, W_WI, W_QB, W_KB, W_VB, W_G, W_G)
N_IN = W_QA + W_KA + W_VA + W_QI + W_KI + W_WI + W_QB + W_KB + W_VB + 2 * W_G

kernel_name = 'hybrid_dsa_stickbreak_peer_stream'


def _rms(x, g):
    xf = x.astype(jnp.float32)
    y = xf * lax.rsqrt(jnp.mean(xf * xf, axis=-1, keepdims=True) + EPS)
    return (y * g.astype(jnp.float32)).astype(x.dtype)


def _rope(x, pos):
    half = x.shape[-1] // 2
    inv = ROPE_THETA ** (-jnp.arange(half, dtype=jnp.float32) / half)
    ang = pos.astype(jnp.float32)[:, None] * inv[None, :]
    cos = jnp.cos(ang)[None, :, None, :]
    sin = jnp.sin(ang)[None, :, None, :]
    xf = x.astype(jnp.float32)
    x1, x2 = xf[..., :half], xf[..., half:]
    return jnp.concatenate([x1 * cos - x2 * sin, x2 * cos + x1 * sin], axis=-1).astype(x.dtype)


def _split_cols(p):
    outs = []
    start = 0
    for w in SPLITS:
        outs.append(p[..., start:start + w])
        start += w
    return outs


def _sweep(fn, qs, q_pos):
    T = q_pos.shape[0]
    if T % Q_BLOCK != 0:
        return fn(qs, q_pos)
    nb = T // Q_BLOCK

    def blk(a):
        return jnp.moveaxis(a.reshape((a.shape[0], nb, Q_BLOCK) + a.shape[2:]), 1, 0)

    out = lax.map(lambda args: fn(args[0], args[1]),
                  (tuple(blk(a) for a in qs), q_pos.reshape(nb, Q_BLOCK)))
    out = jnp.moveaxis(out, 0, 1)
    return out.reshape((out.shape[0], T) + out.shape[3:])


def _dsa_block(qs, q_pos, k, v, ki, k_pos, topk):
    q, qi, wi = qs
    B, Q = q.shape[0], q.shape[1]
    f32 = jnp.float32
    isc = jnp.einsum('bqhd,bsd->bqhs', qi.astype(f32), ki.astype(f32)) * (IDX_DIM ** -0.5)
    isc = jnp.einsum('bqhs,bqh->bqs', jax.nn.relu(isc), wi.astype(f32) * (IDX_HEADS ** -0.5))
    adm = (k_pos[None, :] // CHUNK) <= (q_pos[:, None] // CHUNK)
    isc = jnp.where(adm[None], isc, NEG)
    _, idx = lax.top_k(isc, topk)
    valid = (k_pos[idx] // CHUNK) <= (q_pos[None, :, None] // CHUNK)
    k_sel = jax.vmap(lambda kb, ib: kb[ib])(k, idx)
    v_sel = jax.vmap(lambda vb, ib: vb[ib])(v, idx)
    qg = q.reshape(B, Q, A_KV_HEADS, A_HEADS // A_KV_HEADS, A_HEAD_DIM).astype(f32)
    s = jnp.einsum('bqngd,bqknd->bqngk', qg, k_sel.astype(f32)) * (A_HEAD_DIM ** -0.5)
    s = jnp.where(valid[:, :, None, None, :], s, NEG)
    p = jax.nn.softmax(s, axis=-1)
    o = jnp.einsum('bqngk,bqknd->bqngd', p, v_sel.astype(f32))
    return o.reshape(B, Q, A_HEADS * A_HEAD_DIM).astype(q.dtype)


def _sb_block(qs, q_pos, k, v, k_pos):
    q = qs[0]
    B, Q = q.shape[0], q.shape[1]
    f32 = jnp.float32
    z = jnp.einsum('bqhd,bshd->bhqs', q.astype(f32), k.astype(f32)) * (B_HEAD_DIM ** -0.5)
    before = (k_pos[None, :] < q_pos[:, None])[None, None]
    log_rest = jnp.where(before, jax.nn.log_sigmoid(-z), 0.0)
    after = lax.cumsum(log_rest, axis=3, reverse=True) - log_rest
    att = jnp.where(before, jnp.exp(jax.nn.log_sigmoid(z) + after), 0.0)
    o = jnp.einsum('bhqs,bshd->bqhd', att, v.astype(f32))
    return o.reshape(B, Q, B_HEADS * B_HEAD_DIM).astype(q.dtype)


def _peer(h, w_pq, k1, k2, u_tab, v_tab):
    B, T, D = h.shape
    n = B * T
    xf = h.reshape(n, D)
    f32 = jnp.float32

    def one(xb):
        m = xb.shape[0]
        q = (xb @ w_pq).reshape(m, P_HEADS, 2, P_HALF).astype(f32)
        s1 = jnp.einsum('nhd,hkd->nhk', q[:, :, 0], k1.astype(f32))
        s2 = jnp.einsum('nhd,hkd->nhk', q[:, :, 1], k2.astype(f32))
        v1, i1 = lax.top_k(s1, P_TOPK)
        v2, i2 = lax.top_k(s2, P_TOPK)
        cand = (v1[..., :, None] + v2[..., None, :]).reshape(m, P_HEADS, P_TOPK * P_TOPK)
        cid = (i1[..., :, None] * N_KEYS + i2[..., None, :]).reshape(m, P_HEADS, P_TOPK * P_TOPK)
        sc, pick = lax.top_k(cand, P_TOPK)
        eid = jnp.take_along_axis(cid, pick, axis=-1)
        g = jax.nn.softmax(sc, axis=-1)
        u = u_tab[eid]
        a = jax.nn.gelu(jnp.einsum('nhkd,nd->nhk', u, xb).astype(f32), approximate=False)
        coef = (g * a).astype(xb.dtype)
        return jnp.einsum('nhk,nhkd->nd', coef, v_tab[eid])

    if n % PEER_BLOCK != 0:
        out = one(xf)
    else:
        out = lax.map(one, xf.reshape(n // PEER_BLOCK, PEER_BLOCK, D)).reshape(n, D)
    return out.reshape(B, T, D)


def _layer(x, pos, past, lw, topk):
    (n1, w_in, qn, kn, ikn, w_pa, w_pb, w_o, n2, pwq, pk1, pk2, pu, pv) = lw
    B, T, _ = x.shape
    h = _rms(x, n1)
    qa, ka, va, qi, ki, wi, qb, kb, vb, ga, gb = _split_cols(h @ w_in)
    qa = _rope(_rms(qa.reshape(B, T, A_HEADS, A_HEAD_DIM), qn), pos)
    ka = _rope(_rms(ka.reshape(B, T, A_KV_HEADS, A_HEAD_DIM), kn), pos)
    va = va.reshape(B, T, A_KV_HEADS, A_HEAD_DIM)
    qi = _rope(qi.reshape(B, T, IDX_HEADS, IDX_DIM), pos)
    ki = _rope(_rms(ki, ikn)[:, :, None, :], pos)[:, :, 0, :]
    qb = qb.reshape(B, T, B_HEADS, B_HEAD_DIM)
    kb = kb.reshape(B, T, B_HEADS, B_HEAD_DIM)
    vb = vb.reshape(B, T, B_HEADS, B_HEAD_DIM)
    new = (ka, va, ki, kb, vb)
    if past is None:
        ka_all, va_all, ki_all, kb_all, vb_all = new
        k_pos = pos
    else:
        ka_all = jnp.concatenate([past[0], ka], axis=1)
        va_all = jnp.concatenate([past[1], va], axis=1)
        ki_all = jnp.concatenate([past[2], ki], axis=1)
        kb_all = jnp.concatenate([past[3], kb], axis=1)
        vb_all = jnp.concatenate([past[4], vb], axis=1)
        k_pos = jnp.arange(ka_all.shape[1], dtype=jnp.int32)
    oa = _sweep(lambda qs, qp: _dsa_block(qs, qp, ka_all, va_all, ki_all, k_pos, topk),
                (qa, qi, wi), pos)
    ob = _sweep(lambda qs, qp: _sb_block(qs, qp, kb_all, vb_all, k_pos), (qb,), pos)
    m = jax.nn.sigmoid(ga) * (oa @ w_pa) + jax.nn.sigmoid(gb) * (ob @ w_pb)
    x = x + m @ w_o
    x = x + _peer(_rms(x, n2), pwq, pk1, pk2, pu, pv)
    return x, new


def setup_inputs(seed: int = 0) -> dict:
    key = jax.random.key(seed)
    ks = jax.random.split(key, 24)
    f32 = jnp.float32

    def nrm(k, shape, scale):
        return jax.random.normal(k, shape, f32) * scale

    def gain(k, shape):
        return 1.0 + 0.02 * jax.random.normal(k, shape, f32)

    return {
        'x_prompt': nrm(ks[0], (BATCH, SEQ, D_MODEL), 1.0),
        'x_sample': nrm(ks[1], (DEC_BATCH, DEC_SEQ, D_MODEL), 1.0),
        'cache_a_k': nrm(ks[2], (DEPTH, DEC_BATCH, PAST_LEN, A_KV_HEADS, A_HEAD_DIM), 1.0),
        'cache_a_v': nrm(ks[3], (DEPTH, DEC_BATCH, PAST_LEN, A_KV_HEADS, A_HEAD_DIM), 0.5),
        'cache_idx_k': nrm(ks[4], (DEPTH, DEC_BATCH, PAST_LEN, IDX_DIM), 1.0),
        'cache_b_k': nrm(ks[5], (DEPTH, DEC_BATCH, PAST_LEN, B_HEADS, B_HEAD_DIM), 1.0),
        'cache_b_v': nrm(ks[6], (DEPTH, DEC_BATCH, PAST_LEN, B_HEADS, B_HEAD_DIM), 0.5),
        'norm1': gain(ks[7], (DEPTH, D_MODEL)),
        'w_in': nrm(ks[8], (DEPTH, D_MODEL, N_IN), D_MODEL ** -0.5),
        'q_norm_a': gain(ks[9], (DEPTH, A_HEAD_DIM)),
        'k_norm_a': gain(ks[10], (DEPTH, A_HEAD_DIM)),
        'idx_k_norm': gain(ks[11], (DEPTH, IDX_DIM)),
        'w_pa': nrm(ks[12], (DEPTH, W_QA, D_MODEL), W_QA ** -0.5),
        'w_pb': nrm(ks[13], (DEPTH, W_QB, D_MODEL), W_QB ** -0.5),
        'w_o': nrm(ks[14], (DEPTH, D_MODEL, D_MODEL), D_MODEL ** -0.5),
        'norm2': gain(ks[15], (DEPTH, D_MODEL)),
        'peer_wq': nrm(ks[16], (DEPTH, D_MODEL, P_HEADS * P_KEY_DIM), D_MODEL ** -0.5),
        'peer_k1': nrm(ks[17], (DEPTH, P_HEADS, N_KEYS, P_HALF), P_HALF ** -0.5),
        'peer_k2': nrm(ks[18], (DEPTH, P_HEADS, N_KEYS, P_HALF), P_HALF ** -0.5),
        'peer_u': nrm(ks[19], (DEPTH, N_EXPERTS, D_MODEL), D_MODEL ** -0.5),
        'peer_v': nrm(ks[20], (DEPTH, N_EXPERTS, D_MODEL), (P_HEADS * P_TOPK) ** -0.5),
    }


def reference(x_prompt, x_sample, cache_a_k, cache_a_v, cache_idx_k, cache_b_k, cache_b_v,
              norm1, w_in, q_norm_a, k_norm_a, idx_k_norm, w_pa, w_pb, w_o, norm2,
              peer_wq, peer_k1, peer_k2, peer_u, peer_v):
    t_p = x_prompt.shape[1]
    t_s = x_sample.shape[1]
    past_len = cache_a_k.shape[2]
    pos_p = jnp.arange(t_p, dtype=jnp.int32)
    pos_s = past_len + jnp.arange(t_s, dtype=jnp.int32)
    topk_p = min(TOPK_MAX, t_p // 4)
    topk_s = min(TOPK_MAX, (past_len + t_s) // 4)
    xp, xs = x_prompt, x_sample
    st_p, st_s = [], []
    for l in range(DEPTH):
        lw = (norm1[l], w_in[l], q_norm_a[l], k_norm_a[l], idx_k_norm[l], w_pa[l], w_pb[l],
              w_o[l], norm2[l], peer_wq[l], peer_k1[l], peer_k2[l], peer_u[l], peer_v[l])
        xp, sp = _layer(xp, pos_p, None, lw, topk_p)
        past = (cache_a_k[l], cache_a_v[l], cache_idx_k[l], cache_b_k[l], cache_b_v[l])
        xs, ss = _layer(xs, pos_s, past, lw, topk_s)
        st_p.append(sp)
        st_s.append(ss)
    a_k_p = jnp.stack([s[0] for s in st_p])
    a_v_p = jnp.stack([s[1] for s in st_p])
    idx_k_p = jnp.stack([s[2] for s in st_p])
    b_k_p = jnp.stack([s[3] for s in st_p])
    b_v_p = jnp.stack([s[4] for s in st_p])
    a_k_s = jnp.stack([s[0] for s in st_s])
    a_v_s = jnp.stack([s[1] for s in st_s])
    idx_k_s = jnp.stack([s[2] for s in st_s])
    b_k_s = jnp.stack([s[3] for s in st_s])
    b_v_s = jnp.stack([s[4] for s in st_s])
    return (xp, xs, a_k_p, a_v_p, idx_k_p, b_k_p, b_v_p, a_k_s, a_v_s, idx_k_s, b_k_s, b_v_s)
```

```python
import functools
import math

import jax
import jax.numpy as jnp
import numpy as np
from jax import lax
from jax.experimental import pallas as pl
from jax.experimental.pallas import tpu as pltpu

CHUNK = 64
EPS = 1e-6
ROPE_THETA = 10000.0
NEG = -1e30
A_HEADS, A_KV_HEADS, HD = 8, 2, 64
IDX_HEADS, IDX_DIM = 8, 64
TOPK_MAX = 256
B_HEADS = 8
P_HEADS, N_KEYS, P_HALF, P_TOPK = 8, 128, 64, 16

LANES = 128
VMEM_LIMIT = 56 * 1024 * 1024

MXU_DTYPE = jnp.bfloat16

SEG_QA, SEG_KA, SEG_VA, SEG_QI, SEG_KIWI = 0, 512, 640, 768, 1280
SEG_QB, SEG_KB, SEG_VB, SEG_GA, SEG_GB, N_IN_PAD = 1408, 1920, 2432, 2944, 3968, 4992


def _cparams(sem):
    return pltpu.CompilerParams(dimension_semantics=sem, vmem_limit_bytes=VMEM_LIMIT)


def _dot(a, b):
    return jnp.dot(a, b, preferred_element_type=jnp.float32)


def _rms_rope_head(blk, gain, cos, sin, scale):
    if gain is not None:
        ms = jnp.mean(blk * blk, axis=0, keepdims=True)
        blk = blk * lax.rsqrt(ms + EPS) * gain
    x1, x2 = blk[:32], blk[32:]
    o1 = x1 * cos - x2 * sin
    o2 = x2 * cos + x1 * sin
    if scale != 1.0:
        o1, o2 = o1 * scale, o2 * scale
    return o1, o2


def _in_proj_kernel(x_ref, n1_ref, w_ref, qn_ref, kn_ref, ikn_ref, cos_ref, sin_ref,
                    qa_ref, qi_ref, wi_ref, qb_ref, vat_ref, vbb_ref, sga_ref, sgb_ref,
                    ka_ref, va_ref, ki_ref, kb_ref, vb_ref):
    x = x_ref[...]
    ms = jnp.mean(x * x, axis=0, keepdims=True)
    h = (x * lax.rsqrt(ms + EPS) * n1_ref[...]).astype(MXU_DTYPE)
    cos, sin = cos_ref[...], sin_ref[...]
    tm = x.shape[1]

    def seg(start, size):
        return _dot(w_ref[start:start + size, :], h)

    y = seg(SEG_QA, 512)
    for hh in range(A_HEADS):
        o1, o2 = _rms_rope_head(y[hh * 64:(hh + 1) * 64], qn_ref[...], cos, sin, HD ** -0.5)
        qa_ref[hh * 64:hh * 64 + 32, :] = o1.astype(qa_ref.dtype)
        qa_ref[hh * 64 + 32:(hh + 1) * 64, :] = o2.astype(qa_ref.dtype)
    y = seg(SEG_KA, 256)
    parts = []
    for hh in range(A_KV_HEADS):
        o1, o2 = _rms_rope_head(y[hh * 64:(hh + 1) * 64], kn_ref[...], cos, sin, 1.0)
        parts += [o1, o2]
    ka_t = jnp.concatenate(parts, axis=0)
    ka_ref[...] = ka_t.T
    va_t = y[128:256]
    va_ref[...] = va_t.T
    vat_ref[...] = va_t.astype(vat_ref.dtype)
    y = seg(SEG_QI, 512)
    for hh in range(IDX_HEADS):
        o1, o2 = _rms_rope_head(y[hh * 64:(hh + 1) * 64], None, cos, sin, IDX_DIM ** -0.5)
        qi_ref[hh * 64:hh * 64 + 32, :] = o1.astype(qi_ref.dtype)
        qi_ref[hh * 64 + 32:(hh + 1) * 64, :] = o2.astype(qi_ref.dtype)
    y = seg(SEG_KIWI, 128)
    o1, o2 = _rms_rope_head(y[0:64], ikn_ref[...], cos, sin, 1.0)
    ki_t = jnp.concatenate([o1, o2, jnp.zeros((64, tm), jnp.float32)], axis=0)
    ki_ref[...] = ki_t.T[:, :64]
    wi_ref[...] = y[64:72] * (IDX_HEADS ** -0.5)
    qb_ref[...] = (seg(SEG_QB, 512) * (HD ** -0.5)).astype(qb_ref.dtype)
    kb_ref[...] = seg(SEG_KB, 512).T
    y = seg(SEG_VB, 512)
    vb_ref[...] = y.T
    yb = y.astype(vbb_ref.dtype)
    for c in range(tm // LANES):
        vbb_ref[c] = yb[:, c * LANES:(c + 1) * LANES]
    sga_ref[...] = jax.nn.sigmoid(seg(SEG_GA, 1024))
    sgb_ref[...] = jax.nn.sigmoid(seg(SEG_GB, 1024))


def _in_proj(xT, n1, w_inT, qn, kn, ikn, cosT, sinT, tm):
    d, n = xT.shape
    nt = n // tm
    pos_blocks = cosT.shape[1] // tm
    f32, bf = jnp.float32, MXU_DTYPE
    tok = lambda i: (0, i)
    row = lambda i: (i, 0)
    const = lambda i: (0, 0)
    out_shape = (
        jax.ShapeDtypeStruct((512, n), bf),
        jax.ShapeDtypeStruct((512, n), bf),
        jax.ShapeDtypeStruct((8, n), f32),
        jax.ShapeDtypeStruct((512, n), bf),
        jax.ShapeDtypeStruct((128, n), bf),
        jax.ShapeDtypeStruct((n // LANES, 512, LANES), bf),
        jax.ShapeDtypeStruct((1024, n), f32),
        jax.ShapeDtypeStruct((1024, n), f32),
        jax.ShapeDtypeStruct((n, 128), f32),
        jax.ShapeDtypeStruct((n, 128), f32),
        jax.ShapeDtypeStruct((n, 64), f32),
        jax.ShapeDtypeStruct((n, 512), f32),
        jax.ShapeDtypeStruct((n, 512), f32),
    )
    out_specs = (
        pl.BlockSpec((512, tm), tok), pl.BlockSpec((512, tm), tok), pl.BlockSpec((8, tm), tok),
        pl.BlockSpec((512, tm), tok), pl.BlockSpec((128, tm), tok),
        pl.BlockSpec((tm // LANES, 512, LANES), lambda i: (i, 0, 0)),
        pl.BlockSpec((1024, tm), tok), pl.BlockSpec((1024, tm), tok),
        pl.BlockSpec((tm, 128), row), pl.BlockSpec((tm, 128), row), pl.BlockSpec((tm, 64), row),
        pl.BlockSpec((tm, 512), row), pl.BlockSpec((tm, 512), row),
    )
    in_specs = [
        pl.BlockSpec((d, tm), tok), pl.BlockSpec((d, 1), const), pl.BlockSpec((N_IN_PAD, d), const),
        pl.BlockSpec((64, 1), const), pl.BlockSpec((64, 1), const), pl.BlockSpec((64, 1), const),
        pl.BlockSpec((32, tm), lambda i: (0, i % pos_blocks)),
        pl.BlockSpec((32, tm), lambda i: (0, i % pos_blocks)),
    ]
    return pl.pallas_call(
        _in_proj_kernel, grid=(nt,), in_specs=in_specs, out_specs=out_specs, out_shape=out_shape,
        compiler_params=_cparams(("parallel",)), name="in_proj",
    )(xT, n1, w_inT, qn, kn, ikn, cosT, sinT)


def _float_order_key(x):
    bits = lax.bitcast_convert_type(x, jnp.int32)
    key = jnp.where(bits < 0, bits ^ jnp.int32(0x7FFFFFFF), bits)
    return jnp.where(x == 0.0, jnp.int32(0), key)


def _count(mask):
    return jnp.sum(jnp.where(mask, 1.0, 0.0), axis=0, keepdims=True)


def _dsa_kernel(qa_ref, qi_ref, wi_ref, k_ref, vt_ref, ki_ref, o_ref, *, t_q, q_pos0, s_valid, topk):
    b, qblk = pl.program_id(0), pl.program_id(1)
    s_pad = k_ref.shape[0]
    lane = lax.broadcasted_iota(jnp.int32, (1, LANES), 1)
    if t_q % LANES == 0:
        tok = qblk * LANES + lane
        valid = None
    else:
        tok = lane % t_q
        valid = (lane // t_q) == b
    qpos = q_pos0 + tok
    kpos = lax.broadcasted_iota(jnp.int32, (s_pad, LANES), 0)
    adm = ((kpos // CHUNK) <= (qpos // CHUNK)) & (kpos < s_valid)

    ki = ki_ref[...].astype(MXU_DTYPE)
    isc = jnp.zeros((s_pad, LANES), jnp.float32)
    for hh in range(IDX_HEADS):
        z = _dot(ki, qi_ref[hh * 64:(hh + 1) * 64, :])
        isc = isc + jnp.maximum(z, 0.0) * wi_ref[hh:hh + 1, :]
    key = _float_order_key(jnp.where(adm, isc, NEG))

    kf = float(topk)
    int_min = jnp.int32(-2 ** 31)
    base = jnp.where(_count(key >= 0) >= kf, jnp.int32(0), int_min)

    def bit_step(i, base):
        cand = base + jnp.left_shift(jnp.int32(1), jnp.int32(30) - i)
        return jnp.where(_count(key >= cand) >= kf, cand, base)

    thr = lax.fori_loop(0, 31, bit_step, base)
    gt = key > thr
    tied = key == thr
    need = kf - _count(gt)
    n_tied = _count(tied)

    nbits = int(math.ceil(math.log2(s_pad))) + 1

    def tie_break(_):
        def idx_step(i, lo):
            cand = lo + jnp.left_shift(jnp.int32(1), jnp.int32(nbits - 1) - i)
            return jnp.where(_count(tied & (kpos <= cand)) < need, cand, lo)
        lo = lax.fori_loop(0, nbits, idx_step, jnp.full((1, LANES), -1, jnp.int32))
        return jnp.where(gt | (tied & (kpos <= lo + 1)), 1, 0)

    sel = lax.cond(jnp.max(n_tied - need) > 0.0, tie_break, lambda _: jnp.where(gt | tied, 1, 0), None)
    msk = (sel > 0) & adm

    kb16 = k_ref[...].astype(MXU_DTYPE)
    zeros64 = jnp.zeros((64, LANES), MXU_DTYPE)
    for n in range(A_KV_HEADS):
        qs = []
        for g in range(A_HEADS // A_KV_HEADS):
            hh = n * (A_HEADS // A_KV_HEADS) + g
            qh = qa_ref[hh * 64:(hh + 1) * 64, :]
            qs.append(jnp.concatenate([qh, zeros64] if n == 0 else [zeros64, qh], axis=0))
        s = _dot(kb16, jnp.concatenate(qs, axis=1))
        ps, ls = [], []
        for g in range(A_HEADS // A_KV_HEADS):
            sg = jnp.where(msk, s[:, g * LANES:(g + 1) * LANES], NEG)
            m = jnp.max(sg, axis=0, keepdims=True)
            p = jnp.where(msk, jnp.exp(sg - m), 0.0)
            ls.append(jnp.sum(p, axis=0, keepdims=True))
            ps.append(p.astype(MXU_DTYPE))
        o = _dot(vt_ref[n * 64:(n + 1) * 64, :], jnp.concatenate(ps, axis=1))
        for g in range(A_HEADS // A_KV_HEADS):
            hh = n * (A_HEADS // A_KV_HEADS) + g
            og = (o[:, g * LANES:(g + 1) * LANES] / ls[g]).astype(o_ref.dtype)
            if valid is None:
                o_ref[hh * 64:(hh + 1) * 64, :] = og
            else:
                @pl.when(b == 0)
                def _():
                    o_ref[hh * 64:(hh + 1) * 64, :] = jnp.zeros((64, LANES), o_ref.dtype)
                o_ref[hh * 64:(hh + 1) * 64, :] = jnp.where(valid, og, o_ref[hh * 64:(hh + 1) * 64, :])


def _q_block_index(t_q):
    return lambda b, q: (0, (b * t_q) // LANES + q)


def _dsa(qaT, qiT, wiT, k, vT, ki, *, n_b, t_q, q_pos0, s_valid, topk):
    s_pad = k.shape[1]
    nq = max(1, t_q // LANES)
    qmap = _q_block_index(t_q)
    kern = functools.partial(_dsa_kernel, t_q=t_q, q_pos0=q_pos0, s_valid=s_valid, topk=topk)
    return pl.pallas_call(
        kern, grid=(n_b, nq),
        in_specs=[
            pl.BlockSpec((512, LANES), qmap), pl.BlockSpec((512, LANES), qmap), pl.BlockSpec((8, LANES), qmap),
            pl.BlockSpec((None, s_pad, 128), lambda b, q: (b, 0, 0)),
            pl.BlockSpec((None, 128, s_pad), lambda b, q: (b, 0, 0)),
            pl.BlockSpec((None, s_pad, 64), lambda b, q: (b, 0, 0)),
        ],
        out_specs=pl.BlockSpec((512, LANES), qmap),
        out_shape=jax.ShapeDtypeStruct(qaT.shape, MXU_DTYPE),
        compiler_params=_cparams(("arbitrary", "arbitrary")), name="dsa_attn",
    )(qaT, qiT, wiT, k, vT, ki)


def _sb_kernel(q_ref, k_ref, vb_ref, o_ref, *, t_q, q_pos0, n_kblk_fn):
    b, qblk = pl.program_id(1), pl.program_id(2)
    lane = lax.broadcasted_iota(jnp.int32, (1, LANES), 1)
    if t_q % LANES == 0:
        tok = qblk * LANES + lane
        valid = None
    else:
        tok = lane % t_q
        valid = (lane // t_q) == b
    qpos = q_pos0 + tok
    row = lax.broadcasted_iota(jnp.int32, (LANES, LANES), 0)
    col = lax.broadcasted_iota(jnp.int32, (LANES, LANES), 1)
    tri = jnp.where(col > row, 1.0, 0.0).astype(MXU_DTYPE)
    zeros64 = jnp.zeros((64, LANES), MXU_DTYPE)
    qpad = [jnp.concatenate([q_ref[0:64, :], zeros64], axis=0),
            jnp.concatenate([zeros64, q_ref[64:128, :]], axis=0)]
    n_kblk = n_kblk_fn(qblk)

    def body(i, carry):
        jj = n_kblk - 1 - i
        kblk = k_ref[pl.ds(pl.multiple_of(jj * LANES, LANES), LANES), :].astype(MXU_DTYPE)
        vblk = vb_ref[jj]
        before = (jj * LANES + row) < qpos
        out = []
        for hh in range(2):
            o_acc, r_acc = carry[2 * hh], carry[2 * hh + 1]
            z = _dot(kblk, qpad[hh])
            ls = jnp.minimum(z, 0.0) - jnp.log1p(jnp.exp(-jnp.abs(z)))
            lr = jnp.where(before, ls - z, 0.0)
            hi = lr.astype(MXU_DTYPE)
            r1 = lr - hi.astype(jnp.float32)
            mid = r1.astype(MXU_DTYPE)
            lo = (r1 - mid.astype(jnp.float32)).astype(MXU_DTYPE)
            aft = _dot(tri, jnp.concatenate([hi, mid, lo], axis=1))
            aft = aft[:, :LANES] + aft[:, LANES:2 * LANES] + aft[:, 2 * LANES:]
            att = jnp.where(before, jnp.exp(ls + aft + r_acc), 0.0).astype(MXU_DTYPE)
            o_acc = o_acc + _dot(vblk[hh * 64:(hh + 1) * 64, :], att)
            r_acc = r_acc + jnp.sum(lr, axis=0, keepdims=True)
            out += [o_acc, r_acc]
        return tuple(out)

    z64 = jnp.zeros((64, LANES), jnp.float32)
    z1 = jnp.zeros((1, LANES), jnp.float32)
    res = lax.fori_loop(0, n_kblk, body, (z64, z1, z64, z1))
    for hh in range(2):
        og = res[2 * hh].astype(o_ref.dtype)
        if valid is None:
            o_ref[hh * 64:(hh + 1) * 64, :] = og
        else:
            @pl.when(b == 0)
            def _():
                o_ref[hh * 64:(hh + 1) * 64, :] = jnp.zeros((64, LANES), o_ref.dtype)
            o_ref[hh * 64:(hh + 1) * 64, :] = jnp.where(valid, og, o_ref[hh * 64:(hh + 1) * 64, :])


def _sb(qbT, kb, vbB, *, n_b, t_q, q_pos0):
    s_pad = kb.shape[1]
    nq = max(1, t_q // LANES)
    n_kblk_total = s_pad // LANES
    if t_q % LANES == 0:
        n_kblk_fn = lambda qblk: (q_pos0 // LANES) + qblk + 1
    else:
        n_kblk_fn = lambda qblk: n_kblk_total
    kern = functools.partial(_sb_kernel, t_q=t_q, q_pos0=q_pos0, n_kblk_fn=n_kblk_fn)
    qmap = lambda hp, b, q: (hp, (b * t_q) // LANES + q)
    return pl.pallas_call(
        kern, grid=(B_HEADS // 2, n_b, nq),
        in_specs=[
            pl.BlockSpec((128, LANES), qmap),
            pl.BlockSpec((None, s_pad, 128), lambda hp, b, q: (b, 0, hp)),
            pl.BlockSpec((None, n_kblk_total, 128, LANES), lambda hp, b, q: (b, 0, hp, 0)),
        ],
        out_specs=pl.BlockSpec((128, LANES), qmap),
        out_shape=jax.ShapeDtypeStruct(qbT.shape, MXU_DTYPE),
        compiler_params=_cparams(("arbitrary", "arbitrary", "arbitrary")), name="sb_attn",
    )(qbT, kb, vbB)


def _top16(s, n_rows):
    rid = lax.broadcasted_iota(jnp.int32, s.shape, 0).astype(jnp.float32)
    vals = []
    for _ in range(P_TOPK):
        m = jnp.max(s, axis=0, keepdims=True)
        first = jnp.min(jnp.where(s == m, rid, float(n_rows)), axis=0, keepdims=True)
        vals.append(m)
        s = jnp.where(rid == first, -jnp.inf, s)
    return vals


def _out_proj_kernel(x_ref, oa_ref, ob_ref, sga_ref, sgb_ref, wpa_ref, wpb_ref, wo_ref, n2_ref,
                     wpq_ref, k1_ref, k2_ref,
                     x1_ref, h2_ref, s1_ref, s2_ref, e1_ref, e2_ref, thr_ref, q_scr):
    m = sga_ref[...] * _dot(wpa_ref[...], oa_ref[...]) + sgb_ref[...] * _dot(wpb_ref[...], ob_ref[...])
    x1 = x_ref[...] + _dot(wo_ref[...], m.astype(MXU_DTYPE))
    x1_ref[...] = x1
    ms = jnp.mean(x1 * x1, axis=0, keepdims=True)
    h2 = (x1 * lax.rsqrt(ms + EPS) * n2_ref[...]).astype(MXU_DTYPE)
    h2_ref[...] = h2
    q_scr[...] = _dot(wpq_ref[...], h2).astype(MXU_DTYPE)

    def head(hh, carry):
        r0 = pl.multiple_of(hh * 128, 128)
        s1 = _dot(k1_ref[hh], q_scr[pl.ds(r0, 64), :])
        s2 = _dot(k2_ref[hh], q_scr[pl.ds(pl.multiple_of(r0 + 64, 64), 64), :])
        a1 = _top16(s1, N_KEYS)
        a2 = _top16(s2, N_KEYS)
        a2m = jnp.concatenate(a2, axis=0)
        cand = jnp.concatenate([a + a2m for a in a1], axis=0)
        top = _top16(cand, P_TOPK * P_TOPK)
        mx = top[0]
        z = top[0] * 0.0
        for v in top:
            z = z + jnp.exp(v - mx)
        s1_ref[hh] = s1
        s2_ref[hh] = s2
        e1_ref[hh] = jnp.exp(s1 - a1[0]) / z
        e2_ref[hh] = jnp.exp(s2 - a2[0])
        thr_ref[pl.ds(hh, 1), :] = top[P_TOPK - 1]
        return carry

    lax.fori_loop(0, P_HEADS, head, 0)


def _out_proj(xT, oaT, obT, sgaT, sgbT, w_paT, w_pbT, w_oT, n2, w_pqT, k1, k2, tm):
    d, n = xT.shape
    nt = n // tm
    f32, bf = jnp.float32, MXU_DTYPE
    tok = lambda i: (0, i)
    tok3 = lambda i: (0, 0, i)
    c2 = lambda i: (0, 0)
    c3 = lambda i: (0, 0, 0)
    out_shape = (
        jax.ShapeDtypeStruct((d, n), f32), jax.ShapeDtypeStruct((d, n), bf),
        jax.ShapeDtypeStruct((P_HEADS, N_KEYS, n), f32), jax.ShapeDtypeStruct((P_HEADS, N_KEYS, n), f32),
        jax.ShapeDtypeStruct((P_HEADS, N_KEYS, n), f32), jax.ShapeDtypeStruct((P_HEADS, N_KEYS, n), f32),
        jax.ShapeDtypeStruct((P_HEADS, n), f32),
    )
    out_specs = (
        pl.BlockSpec((d, tm), tok), pl.BlockSpec((d, tm), tok),
        pl.BlockSpec((P_HEADS, N_KEYS, tm), tok3), pl.BlockSpec((P_HEADS, N_KEYS, tm), tok3),
        pl.BlockSpec((P_HEADS, N_KEYS, tm), tok3), pl.BlockSpec((P_HEADS, N_KEYS, tm), tok3),
        pl.BlockSpec((P_HEADS, tm), tok),
    )
    in_specs = [
        pl.BlockSpec((d, tm), tok), pl.BlockSpec((512, tm), tok), pl.BlockSpec((512, tm), tok),
        pl.BlockSpec((d, tm), tok), pl.BlockSpec((d, tm), tok),
        pl.BlockSpec((d, 512), c2), pl.BlockSpec((d, 512), c2), pl.BlockSpec((d, d), c2),
        pl.BlockSpec((d, 1), c2), pl.BlockSpec((P_HEADS * 128, d), c2),
        pl.BlockSpec((P_HEADS, N_KEYS, P_HALF), c3), pl.BlockSpec((P_HEADS, N_KEYS, P_HALF), c3),
    ]
    return pl.pallas_call(
        _out_proj_kernel, grid=(nt,), in_specs=in_specs, out_specs=out_specs, out_shape=out_shape,
        scratch_shapes=[pltpu.VMEM((P_HEADS * 128, tm), bf)],
        compiler_params=_cparams(("parallel",)), name="out_proj",
    )(xT, oaT, obT, sgaT, sgbT, w_paT, w_pbT, w_oT, n2, w_pqT, k1, k2)


def _gelu_exact(x):
    return 0.5 * x * (1.0 + lax.erf(x * np.float32(math.sqrt(0.5))))


def _peer_kernel(h2_ref, u_ref, vt_ref, s1_ref, s2_ref, e1_ref, e2_ref, thr_ref, x1_ref,
                 o_ref, acc_ref, coef_ref, *, te):
    ec = pl.program_id(1)
    tm = h2_ref.shape[1]

    @pl.when(ec == 0)
    def _():
        acc_ref[...] = jnp.zeros_like(acc_ref)

    a_all = _dot(u_ref[...], h2_ref[...])
    assert te == 8 * N_KEYS
    i1_group = pl.ds(pl.multiple_of(ec * 8, 8), 8)
    for c in range(tm // LANES):
        cs = slice(c * LANES, (c + 1) * LANES)
        s1g = [s1_ref[hh, i1_group, cs] for hh in range(P_HEADS)]
        e1g = [e1_ref[hh, i1_group, cs] for hh in range(P_HEADS)]
        for il in range(8):
            gate = jnp.zeros((N_KEYS, LANES), jnp.float32)
            for hh in range(P_HEADS):
                val = s1g[hh][il:il + 1, :] + s2_ref[hh, :, cs]
                g = e1g[hh][il:il + 1, :] * e2_ref[hh, :, cs]
                gate = gate + jnp.where(val >= thr_ref[hh:hh + 1, cs], g, 0.0)
            a = a_all[il * N_KEYS:(il + 1) * N_KEYS, cs]
            coef_ref[il * N_KEYS:(il + 1) * N_KEYS, cs] = (gate * _gelu_exact(a)).astype(coef_ref.dtype)
    acc_ref[...] += _dot(vt_ref[...], coef_ref[...])

    @pl.when(ec == pl.num_programs(1) - 1)
    def _():
        o_ref[...] = x1_ref[...] + acc_ref[...]


def _peer(h2T, u, vT, s1T, s2T, e1T, e2T, thr, x1T, tm, te):
    d, n = x1T.shape
    n_exp = u.shape[0]
    tok = lambda i, e: (0, i)
    tok3 = lambda i, e: (0, 0, i)
    in_specs = [
        pl.BlockSpec((d, tm), tok),
        pl.BlockSpec((te, d), lambda i, e: (e, 0)),
        pl.BlockSpec((d, te), lambda i, e: (0, e)),
        pl.BlockSpec((P_HEADS, N_KEYS, tm), tok3), pl.BlockSpec((P_HEADS, N_KEYS, tm), tok3),
        pl.BlockSpec((P_HEADS, N_KEYS, tm), tok3), pl.BlockSpec((P_HEADS, N_KEYS, tm), tok3),
        pl.BlockSpec((P_HEADS, tm), tok),
        pl.BlockSpec((d, tm), tok),
    ]
    return pl.pallas_call(
        functools.partial(_peer_kernel, te=te), grid=(n // tm, n_exp // te),
        in_specs=in_specs, out_specs=pl.BlockSpec((d, tm), tok),
        out_shape=jax.ShapeDtypeStruct((d, n), jnp.float32),
        scratch_shapes=[pltpu.VMEM((d, tm), jnp.float32), pltpu.VMEM((te, tm), MXU_DTYPE)],
        compiler_params=_cparams(("parallel", "arbitrary")), name="peer_dense",
    )(h2T, u, vT, s1T, s2T, e1T, e2T, thr, x1T)


def _pack_w_in(w_in):
    d = w_in.shape[0]
    cols = [w_in[:, 0:1280], w_in[:, 1280:1352], jnp.zeros((d, 56), w_in.dtype), w_in[:, 1352:]]
    return jnp.concatenate(cols, axis=1).T.astype(MXU_DTYPE)


def _rope_tables(pos):
    half = HD // 2
    inv = ROPE_THETA ** (-jnp.arange(half, dtype=jnp.float32) / half)
    ang = pos.astype(jnp.float32)[None, :] * inv[:, None]
    return jnp.cos(ang), jnp.sin(ang)


def _token_tile(n, pref):
    return pref if n % pref == 0 else LANES


def _layer(xT, lw, rope, n_b, t_q, q_pos0, past, topk):
    (n1, w_inT, qn, kn, ikn, w_paT, w_pbT, w_oT, n2, w_pqT, k1, k2, u, vT) = lw
    n = xT.shape[1]
    tm = _token_tile(n, 256)
    (qaT, qiT, wiT, qbT, vaT, vbB, sgaT, sgbT, ka, va, ki, kb, vb) = _in_proj(
        xT, n1, w_inT, qn, kn, ikn, rope[0], rope[1], tm)
    new = (ka, va, ki, kb, vb)
    if past is None:
        s_valid = t_q
        k_a = ka.reshape(n_b, t_q, 128)
        k_i = ki.reshape(n_b, t_q, 64)
        k_b = kb.reshape(n_b, t_q, 512)
        v_aT = vaT.reshape(128, n_b, t_q).transpose(1, 0, 2)
        v_bB = vbB.reshape(n_b, t_q // LANES, 512, LANES)
    else:
        p_ka, p_vaT, p_ki, p_kb, p_vbT = past
        past_len = p_ka.shape[1]
        s_valid = past_len + t_q
        s_pad = -(-s_valid // LANES) * LANES
        pad = s_pad - s_valid

        def cat_tok(p, new_rows):
            return jnp.pad(jnp.concatenate([p, new_rows.reshape(n_b, t_q, -1)], axis=1), ((0, 0), (0, pad), (0, 0)))

        def cat_feat(pT, newT):
            f = pT.shape[1]
            nt_ = newT.reshape(f, n_b, t_q).transpose(1, 0, 2)
            return jnp.pad(jnp.concatenate([pT, nt_], axis=2), ((0, 0), (0, 0), (0, pad)))

        k_a, k_i, k_b = cat_tok(p_ka, ka), cat_tok(p_ki, ki), cat_tok(p_kb, kb)
        v_aT = cat_feat(p_vaT, vaT)
        v_bT = cat_feat(p_vbT, vb.T.astype(MXU_DTYPE))
        v_bB = v_bT.reshape(n_b, 512, s_pad // LANES, LANES).transpose(0, 2, 1, 3)
    oaT = _dsa(qaT, qiT, wiT, k_a, v_aT, k_i, n_b=n_b, t_q=t_q, q_pos0=q_pos0, s_valid=s_valid, topk=topk)
    obT = _sb(qbT, k_b, v_bB, n_b=n_b, t_q=t_q, q_pos0=q_pos0)
    x1T, h2T, s1T, s2T, e1T, e2T, thr = _out_proj(
        xT, oaT, obT, sgaT, sgbT, w_paT, w_pbT, w_oT, n2, w_pqT, k1, k2, tm)
    tmp = _token_tile(n, 512)
    x2T = _peer(h2T, u, vT, s1T, s2T, e1T, e2T, thr, x1T, tmp, 8 * N_KEYS)
    return x2T, new


def kernel(x_prompt, x_sample, cache_a_k, cache_a_v, cache_idx_k, cache_b_k, cache_b_v, norm1, w_in, q_norm_a, k_norm_a, idx_k_norm, w_pa, w_pb, w_o, norm2, peer_wq, peer_k1, peer_k2, peer_u, peer_v):
    n_bp, t_p, d = x_prompt.shape
    n_bs, t_s, _ = x_sample.shape
    depth = w_in.shape[0]
    past_len = cache_a_k.shape[2]
    assert t_p % LANES == 0 and (n_bs * t_s) % LANES == 0 and LANES % t_s == 0
    topk_p = min(TOPK_MAX, t_p // 4)
    topk_s = min(TOPK_MAX, (past_len + t_s) // 4)
    bf = MXU_DTYPE
    col = lambda g: g.reshape(-1, 1)

    rope_p = _rope_tables(jnp.arange(t_p, dtype=jnp.int32))
    rope_s = _rope_tables(jnp.tile(past_len + jnp.arange(t_s, dtype=jnp.int32), n_bs))

    xpT = x_prompt.reshape(n_bp * t_p, d).T
    xsT = x_sample.reshape(n_bs * t_s, d).T
    st_p, st_s = [], []
    for l in range(depth):
        lw = (col(norm1[l]), _pack_w_in(w_in[l]), col(q_norm_a[l]), col(k_norm_a[l]), col(idx_k_norm[l]),
              w_pa[l].T.astype(bf), w_pb[l].T.astype(bf), w_o[l].T.astype(bf), col(norm2[l]),
              peer_wq[l].T.astype(bf), peer_k1[l].astype(bf), peer_k2[l].astype(bf),
              peer_u[l].astype(bf), peer_v[l].T.astype(bf))
        past = (cache_a_k[l].reshape(n_bs, past_len, -1),
                cache_a_v[l].reshape(n_bs, past_len, -1).transpose(0, 2, 1).astype(bf),
                cache_idx_k[l],
                cache_b_k[l].reshape(n_bs, past_len, -1),
                cache_b_v[l].reshape(n_bs, past_len, -1).transpose(0, 2, 1).astype(bf))
        xpT, sp = _layer(xpT, lw, rope_p, n_bp, t_p, 0, None, topk_p)
        xsT, ss = _layer(xsT, lw, rope_s, n_bs, t_s, past_len, past, topk_s)
        st_p.append(sp)
        st_s.append(ss)

    def stack(sts, j, n_b, t, shape):
        return jnp.stack([s[j] for s in sts]).reshape((depth, n_b, t) + shape)

    outs = [xpT.T.reshape(n_bp, t_p, d), xsT.T.reshape(n_bs, t_s, d)]
    shapes = ((A_KV_HEADS, HD), (A_KV_HEADS, HD), (IDX_DIM,), (B_HEADS, HD), (B_HEADS, HD))
    for sts, n_b, t in ((st_p, n_bp, t_p), (st_s, n_bs, t_s)):
        for j, shp in enumerate(shapes):
            outs.append(stack(sts, j, n_b, t, shp))
    return tuple(outs)
```

```python
import functools
import math

import jax
import jax.numpy as jnp
import numpy as np
from jax import lax
from jax.experimental import pallas as pl
from jax.experimental.pallas import tpu as pltpu

CHUNK = 64
EPS = 1e-6
ROPE_THETA = 10000.0
NEG = -1e30
A_HEADS, A_KV_HEADS, HD = 8, 2, 64
A_GROUP = A_HEADS // A_KV_HEADS
IDX_HEADS, IDX_DIM = 8, 64
TOPK_MAX = 256
B_HEADS = 8
P_HEADS, N_KEYS, P_HALF, P_TOPK = 8, 128, 64, 16

LANES = 128
SUBLANES = 8
VMEM_LIMIT = 56 * 1024 * 1024
TM_PROJ = 256
TM_PEER = 512
TE_PEER = SUBLANES * N_KEYS
KEY_CHUNK = 512
N_EXTENT_CLASSES = 4

MXU_DTYPE = jnp.bfloat16

SEG_QA, SEG_KA, SEG_VA, SEG_QI, SEG_KIWI = 0, 512, 640, 768, 1280
SEG_QB, SEG_KB, SEG_VB, SEG_GA, SEG_GB, N_IN_PAD = 1408, 1920, 2432, 2944, 3968, 4992


def _cparams(sem):
    return pltpu.CompilerParams(dimension_semantics=sem, vmem_limit_bytes=VMEM_LIMIT)


def _dot(a, b):
    return jnp.dot(a, b, preferred_element_type=jnp.float32)


def _col_reduce(x, op):
    rows = x.shape[0]
    if rows % 64 == 0 and rows > 64:
        x = op(x.reshape(rows // 64, 64, x.shape[1]), axis=0)
    return op(x, axis=0, keepdims=True)


def _rms_rope_head(blk, gain, cos, sin, scale):
    if gain is not None:
        ms = jnp.mean(blk * blk, axis=0, keepdims=True)
        blk = blk * lax.rsqrt(ms + EPS) * gain
    x1, x2 = blk[:32], blk[32:]
    o1 = x1 * cos - x2 * sin
    o2 = x2 * cos + x1 * sin
    if scale != 1.0:
        o1, o2 = o1 * scale, o2 * scale
    return o1, o2


def _in_proj_kernel(x_ref, n1_ref, w_ref, qn_ref, kn_ref, ikn_ref, cos_ref, sin_ref,
                    qa_ref, qi_ref, wi_ref, qb_ref, vat_ref, vbt_ref, sga_ref, sgb_ref,
                    ka_ref, va_ref, ki_ref, kb_ref, vb_ref):
    x = x_ref[...]
    ms = jnp.mean(x * x, axis=0, keepdims=True)
    h = (x * lax.rsqrt(ms + EPS) * n1_ref[...]).astype(MXU_DTYPE)
    cos, sin = cos_ref[...], sin_ref[...]
    tm = x.shape[1]

    def seg(start, size):
        return _dot(w_ref[start:start + size, :], h)

    y = seg(SEG_QA, 512)
    for hh in range(A_HEADS):
        o1, o2 = _rms_rope_head(y[hh * 64:(hh + 1) * 64], qn_ref[...], cos, sin, HD ** -0.5)
        qa_ref[hh * 64:hh * 64 + 32, :] = o1.astype(qa_ref.dtype)
        qa_ref[hh * 64 + 32:(hh + 1) * 64, :] = o2.astype(qa_ref.dtype)
    y = seg(SEG_KA, 256)
    parts = []
    for hh in range(A_KV_HEADS):
        o1, o2 = _rms_rope_head(y[hh * 64:(hh + 1) * 64], kn_ref[...], cos, sin, 1.0)
        parts += [o1, o2]
    ka_ref[...] = jnp.concatenate(parts, axis=0).T
    va_t = y[128:256]
    va_ref[...] = va_t.T
    vat_ref[...] = va_t.astype(vat_ref.dtype)
    y = seg(SEG_QI, 512)
    for hh in range(IDX_HEADS):
        o1, o2 = _rms_rope_head(y[hh * 64:(hh + 1) * 64], None, cos, sin, IDX_DIM ** -0.5)
        qi_ref[hh * 64:hh * 64 + 32, :] = o1.astype(qi_ref.dtype)
        qi_ref[hh * 64 + 32:(hh + 1) * 64, :] = o2.astype(qi_ref.dtype)
    y = seg(SEG_KIWI, 128)
    o1, o2 = _rms_rope_head(y[0:64], ikn_ref[...], cos, sin, 1.0)
    ki_t = jnp.concatenate([o1, o2, jnp.zeros((64, tm), jnp.float32)], axis=0)
    ki_ref[...] = ki_t.T[:, :64]
    wi_ref[...] = y[64:72] * (IDX_HEADS ** -0.5)
    qb_ref[...] = (seg(SEG_QB, 512) * (HD ** -0.5)).astype(qb_ref.dtype)
    kb_ref[...] = seg(SEG_KB, 512).T
    y = seg(SEG_VB, 512)
    vb_ref[...] = y.T
    vbt_ref[...] = y.astype(vbt_ref.dtype)
    sga_ref[...] = jax.nn.sigmoid(seg(SEG_GA, 1024))
    sgb_ref[...] = jax.nn.sigmoid(seg(SEG_GB, 1024))


def _in_proj(xT, n1, w_inT, qn, kn, ikn, cosT, sinT, tm, n_bo):
    d, n = xT.shape
    nt = n // tm
    t_o = n // n_bo
    tpb = t_o // tm
    pos_blocks = cosT.shape[1] // tm
    f32, bf = jnp.float32, MXU_DTYPE
    tok = lambda i: (0, i)
    row = lambda i: (i, 0)
    const = lambda i: (0, 0)
    slab = lambda i: (i // tpb, 0, i % tpb)
    fm = lambda f, dt: jax.ShapeDtypeStruct((n_bo, f, t_o), dt)
    out_shape = (
        fm(512, bf), fm(512, bf), fm(8, f32), fm(512, bf), fm(128, bf), fm(512, bf),
        jax.ShapeDtypeStruct((1024, n), f32),
        jax.ShapeDtypeStruct((1024, n), f32),
        jax.ShapeDtypeStruct((n, 128), f32),
        jax.ShapeDtypeStruct((n, 128), f32),
        jax.ShapeDtypeStruct((n, 64), f32),
        jax.ShapeDtypeStruct((n, 512), f32),
        jax.ShapeDtypeStruct((n, 512), f32),
    )
    out_specs = (
        pl.BlockSpec((None, 512, tm), slab), pl.BlockSpec((None, 512, tm), slab), pl.BlockSpec((None, 8, tm), slab),
        pl.BlockSpec((None, 512, tm), slab), pl.BlockSpec((None, 128, tm), slab), pl.BlockSpec((None, 512, tm), slab),
        pl.BlockSpec((1024, tm), tok), pl.BlockSpec((1024, tm), tok),
        pl.BlockSpec((tm, 128), row), pl.BlockSpec((tm, 128), row), pl.BlockSpec((tm, 64), row),
        pl.BlockSpec((tm, 512), row), pl.BlockSpec((tm, 512), row),
    )
    in_specs = [
        pl.BlockSpec((d, tm), tok), pl.BlockSpec((d, 1), const), pl.BlockSpec((N_IN_PAD, d), const),
        pl.BlockSpec((64, 1), const), pl.BlockSpec((64, 1), const), pl.BlockSpec((64, 1), const),
        pl.BlockSpec((32, tm), lambda i: (0, i % pos_blocks)),
        pl.BlockSpec((32, tm), lambda i: (0, i % pos_blocks)),
    ]
    return pl.pallas_call(
        _in_proj_kernel, grid=(nt,), in_specs=in_specs, out_specs=out_specs, out_shape=out_shape,
        compiler_params=_cparams(("parallel",)), name="in_proj",
    )(xT, n1, w_inT, qn, kn, ikn, cosT, sinT)


def _extent_classes(t_q, s_total):
    if t_q % LANES:
        return [(0, 1, s_total)]
    nq = t_q // LANES
    cq = max(1, nq // N_EXTENT_CLASSES)
    assert nq % cq == 0
    return [(c * cq, cq, (c + 1) * cq * LANES) for c in range(nq // cq)]


def _query_positions(b, qblk, *, t_q, q_pos0, q_blk0):
    lane = lax.broadcasted_iota(jnp.int32, (1, LANES), 1)
    if t_q % LANES == 0:
        return q_pos0 + (q_blk0 + qblk) * LANES + lane, None
    return q_pos0 + lane % t_q, (lane // t_q) == b


def _store_queries(o_ref, rows, val, valid, b):
    if valid is None:
        o_ref[rows, :] = val
    else:
        @pl.when(b == 0)
        def _():
            o_ref[rows, :] = jnp.zeros(val.shape, o_ref.dtype)
        o_ref[rows, :] = jnp.where(valid, val, o_ref[rows, :])


def _float_order_key(x):
    bits = lax.bitcast_convert_type(x, jnp.int32)
    key = jnp.where(bits < 0, bits ^ jnp.int32(0x7FFFFFFF), bits)
    return jnp.where(x == 0.0, jnp.int32(0), key)


def _count(mask):
    return _col_reduce(jnp.where(mask, 1.0, 0.0), jnp.sum)


def _dsa_kernel(qa_ref, qi_ref, wi_ref, k_ref, vt_ref, ki_ref, o_ref, *, t_q, q_pos0, q_blk0, s_valid, topk):
    b, qblk = pl.program_id(0), pl.program_id(1)
    s_pad = k_ref.shape[0]
    qpos, valid = _query_positions(b, qblk, t_q=t_q, q_pos0=q_pos0, q_blk0=q_blk0)
    kpos = lax.broadcasted_iota(jnp.int32, (s_pad, LANES), 0)
    adm = ((kpos // CHUNK) <= (qpos // CHUNK)) & (kpos < s_valid)

    qcat = jnp.concatenate([qi_ref[hh * 64:(hh + 1) * 64, :] for hh in range(IDX_HEADS)], axis=1)
    parts = []
    for r0 in range(0, s_pad, KEY_CHUNK):
        r1 = min(s_pad, r0 + KEY_CHUNK)
        z = _dot(ki_ref[r0:r1, :].astype(MXU_DTYPE), qcat)
        acc = jnp.maximum(z[:, :LANES], 0.0) * wi_ref[0:1, :]
        for hh in range(1, IDX_HEADS):
            acc = acc + jnp.maximum(z[:, hh * LANES:(hh + 1) * LANES], 0.0) * wi_ref[hh:hh + 1, :]
        parts.append(acc)
    isc = jnp.concatenate(parts, axis=0) if len(parts) > 1 else parts[0]
    key = _float_order_key(jnp.where(adm, isc, NEG))

    kf = float(topk)
    int_min = jnp.int32(-2 ** 31)
    base = jnp.where(_count(key >= 0) >= kf, jnp.int32(0), int_min)

    def bit_step(i, base):
        cand = base + jnp.left_shift(jnp.int32(1), jnp.int32(30) - i)
        return jnp.where(_count(key >= cand) >= kf, cand, base)

    thr = lax.fori_loop(0, 31, bit_step, base)
    gt = key > thr
    tied = key == thr
    need = kf - _count(gt)
    n_tied = _count(tied)
    nbits = int(math.ceil(math.log2(s_pad))) + 1

    def tie_break(_):
        def idx_step(i, lo):
            cand = lo + jnp.left_shift(jnp.int32(1), jnp.int32(nbits - 1) - i)
            return jnp.where(_count(tied & (kpos <= cand)) < need, cand, lo)
        lo = lax.fori_loop(0, nbits, idx_step, jnp.full((1, LANES), -1, jnp.int32))
        return jnp.where(gt | (tied & (kpos <= lo + 1)), 1, 0)

    sel = lax.cond(jnp.max(n_tied - need) > 0.0, tie_break, lambda _: jnp.where(gt | tied, 1, 0), None)
    bias = jnp.where((sel > 0) & adm, 0.0, NEG)

    kb16 = k_ref[...].astype(MXU_DTYPE)
    zeros64 = jnp.zeros((64, LANES), MXU_DTYPE)
    for n in range(A_KV_HEADS):
        qs = []
        for g in range(A_GROUP):
            hh = n * A_GROUP + g
            qh = qa_ref[hh * 64:(hh + 1) * 64, :]
            qs.append(jnp.concatenate([qh, zeros64] if n == 0 else [zeros64, qh], axis=0))
        s = _dot(kb16, jnp.concatenate(qs, axis=1))
        ps, ls = [], []
        for g in range(A_GROUP):
            sg = s[:, g * LANES:(g + 1) * LANES] + bias
            p = jnp.exp(sg - _col_reduce(sg, jnp.max))
            ls.append(_col_reduce(p, jnp.sum))
            ps.append(p.astype(MXU_DTYPE))
        o = _dot(vt_ref[n * 64:(n + 1) * 64, :], jnp.concatenate(ps, axis=1))
        for g in range(A_GROUP):
            hh = n * A_GROUP + g
            og = (o[:, g * LANES:(g + 1) * LANES] / ls[g]).astype(o_ref.dtype)
            _store_queries(o_ref, slice(hh * 64, (hh + 1) * 64), og, valid, b)


def _dsa(qaT, qiT, wiT, k, vT, ki, *, n_b, t_q, q_pos0, s_valid, topk):
    shared = t_q % LANES != 0
    outs = []
    for q_blk0, n_qblk, s_used in _extent_classes(t_q, k.shape[1]):
        qmap = lambda b, q, q0=q_blk0: (0 if shared else b, 0, q0 + q)
        omap = lambda b, q: (0 if shared else b, 0, q)
        kern = functools.partial(_dsa_kernel, t_q=t_q, q_pos0=q_pos0, q_blk0=q_blk0, s_valid=s_valid, topk=topk)
        outs.append(pl.pallas_call(
            kern, grid=(n_b, n_qblk),
            in_specs=[
                pl.BlockSpec((None, 512, LANES), qmap), pl.BlockSpec((None, 512, LANES), qmap),
                pl.BlockSpec((None, 8, LANES), qmap),
                pl.BlockSpec((None, s_used, 128), lambda b, q: (b, 0, 0)),
                pl.BlockSpec((None, 128, s_used), lambda b, q: (b, 0, 0)),
                pl.BlockSpec((None, s_used, 64), lambda b, q: (b, 0, 0)),
            ],
            out_specs=pl.BlockSpec((None, 512, LANES), omap),
            out_shape=jax.ShapeDtypeStruct((qaT.shape[0], 512, n_qblk * LANES), MXU_DTYPE),
            compiler_params=_cparams(("arbitrary", "arbitrary")), name="dsa_attn",
        )(qaT, qiT, wiT, k, vT, ki))
    return outs[0] if len(outs) == 1 else jnp.concatenate(outs, axis=2)


def _sb_kernel(q_ref, k_ref, vt_ref, o_ref, *, t_q, q_pos0, q_blk0):
    b, qblk = pl.program_id(1), pl.program_id(2)
    s_used = k_ref.shape[0]
    n_blk = s_used // LANES
    qpos, valid = _query_positions(b, qblk, t_q=t_q, q_pos0=q_pos0, q_blk0=q_blk0)
    row = lax.broadcasted_iota(jnp.int32, (LANES, LANES), 0)
    col = lax.broadcasted_iota(jnp.int32, (LANES, LANES), 1)
    tri = jnp.where(col > row, 1.0, 0.0).astype(MXU_DTYPE)
    zeros64 = jnp.zeros((64, LANES), MXU_DTYPE)
    k16 = k_ref[...].astype(MXU_DTYPE)
    for hh in range(2):
        qh = q_ref[hh * 64:(hh + 1) * 64, :]
        z_all = _dot(k16, jnp.concatenate([qh, zeros64] if hh == 0 else [zeros64, qh], axis=0))
        blocks, tots = [], []
        for c in range(n_blk):
            z = z_all[c * LANES:(c + 1) * LANES]
            before = (c * LANES + row) < qpos
            ls = jnp.minimum(z, 0.0) - jnp.log1p(jnp.exp(-jnp.abs(z)))
            lr = jnp.where(before, ls - z, 0.0)
            hi = lr.astype(MXU_DTYPE)
            r1 = lr - hi.astype(jnp.float32)
            mid = r1.astype(MXU_DTYPE)
            lo = (r1 - mid.astype(jnp.float32)).astype(MXU_DTYPE)
            aft = _dot(tri, jnp.concatenate([hi, mid, lo], axis=1))
            aft = aft[:, :LANES] + aft[:, LANES:2 * LANES] + aft[:, 2 * LANES:]
            blocks.append((ls + aft, before))
            tots.append(jnp.sum(lr, axis=0, keepdims=True))
        rest = jnp.zeros((1, LANES), jnp.float32)
        atts = [None] * n_blk
        for c in reversed(range(n_blk)):
            arg, before = blocks[c]
            atts[c] = jnp.where(before, jnp.exp(arg + rest), 0.0).astype(MXU_DTYPE)
            rest = rest + tots[c]
        att = jnp.concatenate(atts, axis=0) if n_blk > 1 else atts[0]
        og = _dot(vt_ref[hh * 64:(hh + 1) * 64, :], att).astype(o_ref.dtype)
        _store_queries(o_ref, slice(hh * 64, (hh + 1) * 64), og, valid, b)


def _sb(qbT, kb, vbT, *, n_b, t_q, q_pos0):
    shared = t_q % LANES != 0
    outs = []
    for q_blk0, n_qblk, s_used in _extent_classes(t_q, kb.shape[1]):
        qmap = lambda hp, b, q, q0=q_blk0: (0 if shared else b, hp, q0 + q)
        omap = lambda hp, b, q: (0 if shared else b, hp, q)
        kern = functools.partial(_sb_kernel, t_q=t_q, q_pos0=q_pos0, q_blk0=q_blk0)
        outs.append(pl.pallas_call(
            kern, grid=(B_HEADS // 2, n_b, n_qblk),
            in_specs=[
                pl.BlockSpec((None, 128, LANES), qmap),
                pl.BlockSpec((None, s_used, 128), lambda hp, b, q: (b, 0, hp)),
                pl.BlockSpec((None, 128, s_used), lambda hp, b, q: (b, hp, 0)),
            ],
            out_specs=pl.BlockSpec((None, 128, LANES), omap),
            out_shape=jax.ShapeDtypeStruct((qbT.shape[0], 512, n_qblk * LANES), MXU_DTYPE),
            compiler_params=_cparams(("arbitrary", "arbitrary", "arbitrary")), name="sb_attn",
        )(qbT, kb, vbT))
    return outs[0] if len(outs) == 1 else jnp.concatenate(outs, axis=2)


def _top_values(s, n):
    rows = s.shape[0]
    rid = lax.broadcasted_iota(jnp.int32, s.shape, 0).astype(jnp.float32)
    vals = []
    for _ in range(n):
        m = jnp.max(s, axis=0, keepdims=True)
        first = jnp.min(jnp.where(s == m, rid, float(rows)), axis=0, keepdims=True)
        vals.append(m)
        s = jnp.where(rid == first, -jnp.inf, s)
    return vals


def _candidate_sums(a1, a2):
    tm = a1[0].shape[1]
    ninf = jnp.full((1, tm), -jnp.inf, jnp.float32)
    pad = [ninf] * (3 * SUBLANES - (P_TOPK + 1))
    a1m = jnp.concatenate(a1 + pad, axis=0)
    a2m = jnp.concatenate(a2 + pad, axis=0)
    r8 = lax.broadcasted_iota(jnp.int32, (SUBLANES, tm), 0)
    pieces = [a1[0] + a2m, a1[1] + a2m[0:SUBLANES]]
    for i in range(2, SUBLANES):
        pieces.append(jnp.where(r8 < (P_TOPK + 1) // (i + 1), a1[i] + a2m[0:SUBLANES], -jnp.inf))
    pieces.append(a1m[SUBLANES:] + a2[0])
    return jnp.concatenate(pieces, axis=0)


def _out_proj_kernel(x_ref, oa_ref, ob_ref, sga_ref, sgb_ref, wpa_ref, wpb_ref, wo_ref, n2_ref,
                     wpq_ref, k1_ref, k2_ref,
                     x1_ref, h2_ref, e1_ref, e2_ref, gthr_ref, q_scr):
    m = sga_ref[...] * _dot(wpa_ref[...], oa_ref[...]) + sgb_ref[...] * _dot(wpb_ref[...], ob_ref[...])
    x1 = x_ref[...] + _dot(wo_ref[...], m.astype(MXU_DTYPE))
    x1_ref[...] = x1
    ms = jnp.mean(x1 * x1, axis=0, keepdims=True)
    h2 = (x1 * lax.rsqrt(ms + EPS) * n2_ref[...]).astype(MXU_DTYPE)
    h2_ref[...] = h2
    q_scr[...] = _dot(wpq_ref[...], h2).astype(MXU_DTYPE)

    def head(hh):
        r0 = pl.multiple_of(hh * 128, 128)
        s1 = _dot(k1_ref[hh], q_scr[pl.ds(r0, 64), :])
        s2 = _dot(k2_ref[hh], q_scr[pl.ds(pl.multiple_of(r0 + 64, 64), 64), :])
        a1 = _top_values(s1, P_TOPK + 1)
        a2 = _top_values(s2, P_TOPK + 1)
        top = _top_values(_candidate_sums(a1, a2), P_TOPK + 1)
        mx = top[0]
        z = jnp.zeros_like(mx)
        for v in top[:P_TOPK]:
            z = z + jnp.exp(v - mx)
        rz = 1.0 / z
        e1_ref[hh] = jnp.exp(s1 - a1[0]) * rz
        e2_ref[hh] = jnp.exp(s2 - a2[0])
        gthr_ref[pl.ds(hh, 1), :] = jnp.exp(0.5 * (top[P_TOPK - 1] + top[P_TOPK]) - mx) * rz

    def head_pair(i, carry):
        head(2 * i)
        head(2 * i + 1)
        return carry

    lax.fori_loop(0, P_HEADS // 2, head_pair, 0)


def _out_proj(xT, oaT, obT, sgaT, sgbT, w_paT, w_pbT, w_oT, n2, w_pqT, k1, k2, tm):
    d, n = xT.shape
    nt = n // tm
    tpb = oaT.shape[2] // tm
    f32, bf = jnp.float32, MXU_DTYPE
    tok = lambda i: (0, i)
    tok3 = lambda i: (0, 0, i)
    slab = lambda i: (i // tpb, 0, i % tpb)
    c2 = lambda i: (0, 0)
    c3 = lambda i: (0, 0, 0)
    out_shape = (
        jax.ShapeDtypeStruct((d, n), f32), jax.ShapeDtypeStruct((d, n), bf),
        jax.ShapeDtypeStruct((P_HEADS, N_KEYS, n), f32), jax.ShapeDtypeStruct((P_HEADS, N_KEYS, n), f32),
        jax.ShapeDtypeStruct((P_HEADS, n), f32),
    )
    out_specs = (
        pl.BlockSpec((d, tm), tok), pl.BlockSpec((d, tm), tok),
        pl.BlockSpec((P_HEADS, N_KEYS, tm), tok3), pl.BlockSpec((P_HEADS, N_KEYS, tm), tok3),
        pl.BlockSpec((P_HEADS, tm), tok),
    )
    in_specs = [
        pl.BlockSpec((d, tm), tok), pl.BlockSpec((None, 512, tm), slab), pl.BlockSpec((None, 512, tm), slab),
        pl.BlockSpec((d, tm), tok), pl.BlockSpec((d, tm), tok),
        pl.BlockSpec((d, 512), c2), pl.BlockSpec((d, 512), c2), pl.BlockSpec((d, d), c2),
        pl.BlockSpec((d, 1), c2), pl.BlockSpec((P_HEADS * 128, d), c2),
        pl.BlockSpec((P_HEADS, N_KEYS, P_HALF), c3), pl.BlockSpec((P_HEADS, N_KEYS, P_HALF), c3),
    ]
    return pl.pallas_call(
        _out_proj_kernel, grid=(nt,), in_specs=in_specs, out_specs=out_specs, out_shape=out_shape,
        scratch_shapes=[pltpu.VMEM((P_HEADS * 128, tm), bf)],
        compiler_params=_cparams(("parallel",)), name="out_proj",
    )(xT, oaT, obT, sgaT, sgbT, w_paT, w_pbT, w_oT, n2, w_pqT, k1, k2)


def _gelu_exact(x):
    return 0.5 * x * (1.0 + lax.erf(x * np.float32(math.sqrt(0.5))))


def _peer_kernel(h2_ref, u_ref, vt_ref, e1_ref, e2_ref, gthr_ref, x1_ref, o_ref, acc_ref, coef_ref):
    ec = pl.program_id(1)
    tm = h2_ref.shape[1]

    @pl.when(ec == 0)
    def _():
        acc_ref[...] = jnp.zeros_like(acc_ref)

    a_all = _dot(u_ref[...], h2_ref[...])
    i1_group = pl.ds(pl.multiple_of(ec * SUBLANES, SUBLANES), SUBLANES)
    for c in range(tm // LANES):
        cs = slice(c * LANES, (c + 1) * LANES)
        e1g = [e1_ref[hh, i1_group, cs] for hh in range(P_HEADS)]
        gth = [gthr_ref[hh:hh + 1, cs] for hh in range(P_HEADS)]
        for il in range(SUBLANES):
            gate = jnp.zeros((N_KEYS, LANES), jnp.float32)
            for hh in range(P_HEADS):
                g = e1g[hh][il:il + 1, :] * e2_ref[hh, :, cs]
                gate = gate + jnp.where(g >= gth[hh], g, 0.0)
            a = a_all[il * N_KEYS:(il + 1) * N_KEYS, cs]
            coef_ref[il * N_KEYS:(il + 1) * N_KEYS, cs] = (gate * _gelu_exact(a)).astype(coef_ref.dtype)
    acc_ref[...] += _dot(vt_ref[...], coef_ref[...])

    @pl.when(ec == pl.num_programs(1) - 1)
    def _():
        o_ref[...] = x1_ref[...] + acc_ref[...]


def _peer(h2T, u, vT, e1T, e2T, gthr, x1T, tm):
    d, n = x1T.shape
    n_exp = u.shape[0]
    tok = lambda i, e: (0, i)
    tok3 = lambda i, e: (0, 0, i)
    in_specs = [
        pl.BlockSpec((d, tm), tok),
        pl.BlockSpec((TE_PEER, d), lambda i, e: (e, 0)),
        pl.BlockSpec((d, TE_PEER), lambda i, e: (0, e)),
        pl.BlockSpec((P_HEADS, N_KEYS, tm), tok3), pl.BlockSpec((P_HEADS, N_KEYS, tm), tok3),
        pl.BlockSpec((P_HEADS, tm), tok),
        pl.BlockSpec((d, tm), tok),
    ]
    return pl.pallas_call(
        _peer_kernel, grid=(n // tm, n_exp // TE_PEER),
        in_specs=in_specs, out_specs=pl.BlockSpec((d, tm), tok),
        out_shape=jax.ShapeDtypeStruct((d, n), jnp.float32),
        scratch_shapes=[pltpu.VMEM((d, tm), jnp.float32), pltpu.VMEM((TE_PEER, tm), MXU_DTYPE)],
        compiler_params=_cparams(("parallel", "arbitrary")), name="peer_dense",
    )(h2T, u, vT, e1T, e2T, gthr, x1T)


def _pack_w_in(w_in):
    d = w_in.shape[0]
    cols = [w_in[:, 0:1280], w_in[:, 1280:1352], jnp.zeros((d, 56), w_in.dtype), w_in[:, 1352:]]
    return jnp.concatenate(cols, axis=1).T.astype(MXU_DTYPE)


def _rope_tables(pos):
    half = HD // 2
    inv = ROPE_THETA ** (-jnp.arange(half, dtype=jnp.float32) / half)
    ang = pos.astype(jnp.float32)[None, :] * inv[:, None]
    return jnp.cos(ang), jnp.sin(ang)


def _token_tile(n, pref):
    return pref if n % pref == 0 else LANES


def _layer(xT, lw, rope, n_b, t_q, q_pos0, past, topk):
    (n1, w_inT, qn, kn, ikn, w_paT, w_pbT, w_oT, n2, w_pqT, k1, k2, u, vT) = lw
    n = xT.shape[1]
    tm = _token_tile(n, TM_PROJ)
    n_bo = n_b if past is None else 1
    (qaT, qiT, wiT, qbT, vaT, vbT, sgaT, sgbT, ka, va, ki, kb, vb) = _in_proj(
        xT, n1, w_inT, qn, kn, ikn, rope[0], rope[1], tm, n_bo)
    new = (ka, va, ki, kb, vb)
    if past is None:
        s_valid = t_q
        k_a = ka.reshape(n_b, t_q, 128)
        k_i = ki.reshape(n_b, t_q, 64)
        k_b = kb.reshape(n_b, t_q, 512)
        v_aT, v_bT = vaT, vbT
    else:
        p_ka, p_vaT, p_ki, p_kb, p_vbT = past
        s_valid = p_ka.shape[1] + t_q
        pad = -s_valid % LANES

        def cat_tok(p, new_rows):
            return jnp.pad(jnp.concatenate([p, new_rows.reshape(n_b, t_q, -1)], axis=1), ((0, 0), (0, pad), (0, 0)))

        def cat_feat(pT, newT):
            f = pT.shape[1]
            nt_ = newT.reshape(f, n_b, t_q).transpose(1, 0, 2)
            return jnp.pad(jnp.concatenate([pT, nt_], axis=2), ((0, 0), (0, 0), (0, pad)))

        k_a, k_i, k_b = cat_tok(p_ka, ka), cat_tok(p_ki, ki), cat_tok(p_kb, kb)
        v_aT, v_bT = cat_feat(p_vaT, vaT), cat_feat(p_vbT, vbT)
    oaT = _dsa(qaT, qiT, wiT, k_a, v_aT, k_i, n_b=n_b, t_q=t_q, q_pos0=q_pos0, s_valid=s_valid, topk=topk)
    obT = _sb(qbT, k_b, v_bT, n_b=n_b, t_q=t_q, q_pos0=q_pos0)
    x1T, h2T, e1T, e2T, gthr = _out_proj(
        xT, oaT, obT, sgaT, sgbT, w_paT, w_pbT, w_oT, n2, w_pqT, k1, k2, tm)
    x2T = _peer(h2T, u, vT, e1T, e2T, gthr, x1T, _token_tile(n, TM_PEER))
    return x2T, new


def kernel(x_prompt, x_sample, cache_a_k, cache_a_v, cache_idx_k, cache_b_k, cache_b_v, norm1, w_in, q_norm_a, k_norm_a, idx_k_norm, w_pa, w_pb, w_o, norm2, peer_wq, peer_k1, peer_k2, peer_u, peer_v):
    n_bp, t_p, d = x_prompt.shape
    n_bs, t_s, _ = x_sample.shape
    depth = w_in.shape[0]
    past_len = cache_a_k.shape[2]
    assert t_p % LANES == 0 and n_bs * t_s == LANES and peer_u.shape[1] % TE_PEER == 0
    topk_p = min(TOPK_MAX, t_p // 4)
    topk_s = min(TOPK_MAX, (past_len + t_s) // 4)
    bf = MXU_DTYPE
    col = lambda g: g.reshape(-1, 1)

    rope_p = _rope_tables(jnp.arange(t_p, dtype=jnp.int32))
    rope_s = _rope_tables(jnp.tile(past_len + jnp.arange(t_s, dtype=jnp.int32), n_bs))

    xpT = x_prompt.reshape(n_bp * t_p, d).T
    xsT = x_sample.reshape(n_bs * t_s, d).T
    st_p, st_s = [], []
    for l in range(depth):
        lw = (col(norm1[l]), _pack_w_in(w_in[l]), col(q_norm_a[l]), col(k_norm_a[l]), col(idx_k_norm[l]),
              w_pa[l].T.astype(bf), w_pb[l].T.astype(bf), w_o[l].T.astype(bf), col(norm2[l]),
              peer_wq[l].T.astype(bf), peer_k1[l].astype(bf), peer_k2[l].astype(bf),
              peer_u[l].astype(bf), peer_v[l].T.astype(bf))
        past = (cache_a_k[l].reshape(n_bs, past_len, -1),
                cache_a_v[l].reshape(n_bs, past_len, -1).transpose(0, 2, 1).astype(bf),
                cache_idx_k[l],
                cache_b_k[l].reshape(n_bs, past_len, -1),
                cache_b_v[l].reshape(n_bs, past_len, -1).transpose(0, 2, 1).astype(bf))
        xpT, sp = _layer(xpT, lw, rope_p, n_bp, t_p, 0, None, topk_p)
        xsT, ss = _layer(xsT, lw, rope_s, n_bs, t_s, past_len, past, topk_s)
        st_p.append(sp)
        st_s.append(ss)

    def stack(sts, j, n_b, t, shape):
        return jnp.stack([s[j] for s in sts]).reshape((depth, n_b, t) + shape)

    outs = [xpT.T.reshape(n_bp, t_p, d), xsT.T.reshape(n_bs, t_s, d)]
    shapes = ((A_KV_HEADS, HD), (A_KV_HEADS, HD), (IDX_DIM,), (B_HEADS, HD), (B_HEADS, HD))
    for sts, n_b, t in ((st_p, n_bp, t_p), (st_s, n_bs, t_s)):
        for j, shp in enumerate(shapes):
            outs.append(stack(sts, j, n_b, t, shp))
    return tuple(outs)
```

```python
import functools
import math

import jax
import jax.numpy as jnp
import numpy as np
from jax import lax
from jax.experimental import pallas as pl
from jax.experimental.pallas import tpu as pltpu

CHUNK = 64
EPS = 1e-6
ROPE_THETA = 10000.0
NEG = -1e30
A_HEADS, A_KV_HEADS, HD = 8, 2, 64
A_GROUP = A_HEADS // A_KV_HEADS
IDX_HEADS, IDX_DIM = 8, 64
TOPK_MAX = 256
B_HEADS = 8
P_HEADS, N_KEYS, P_HALF, P_TOPK = 8, 128, 64, 16

LANES = 128
SUBLANES = 8
VMEM_LIMIT = 56 * 1024 * 1024
TM_PROJ = 256
TM_PEER = 512
TE_PEER = SUBLANES * N_KEYS
KEY_CHUNK = 512
N_EXTENT_CLASSES = 4

MXU_DTYPE = jnp.bfloat16

SEG_QA, SEG_KA, SEG_VA, SEG_QI, SEG_KIWI = 0, 512, 640, 768, 1280
SEG_QB, SEG_KB, SEG_VB, SEG_GA, SEG_GB, N_IN_PAD = 1408, 1920, 2432, 2944, 3968, 4992


def _cparams(sem):
    return pltpu.CompilerParams(dimension_semantics=sem, vmem_limit_bytes=VMEM_LIMIT)


def _dot(a, b):
    return jnp.dot(a, b, preferred_element_type=jnp.float32)


def _col_reduce(x, op):
    rows = x.shape[0]
    if rows % 64 == 0 and rows > 64:
        x = op(x.reshape(rows // 64, 64, x.shape[1]), axis=0)
    return op(x, axis=0, keepdims=True)


def _rms_rope_head(blk, gain, cos, sin, scale):
    if gain is not None:
        ms = jnp.mean(blk * blk, axis=0, keepdims=True)
        blk = blk * lax.rsqrt(ms + EPS) * gain
    x1, x2 = blk[:32], blk[32:]
    o1 = x1 * cos - x2 * sin
    o2 = x2 * cos + x1 * sin
    if scale != 1.0:
        o1, o2 = o1 * scale, o2 * scale
    return o1, o2


def _in_proj_kernel(x_ref, n1_ref, w_ref, qn_ref, kn_ref, ikn_ref, cos_ref, sin_ref,
                    qa_ref, qi_ref, wi_ref, qb_ref, vat_ref, vbt_ref, sga_ref, sgb_ref,
                    ka_ref, va_ref, ki_ref, kb_ref, vb_ref, *, x_token_major):
    x = x_ref[...].T if x_token_major else x_ref[...]
    ms = jnp.mean(x * x, axis=0, keepdims=True)
    h = (x * lax.rsqrt(ms + EPS) * n1_ref[...]).astype(MXU_DTYPE)
    cos, sin = cos_ref[...], sin_ref[...]
    tm = x.shape[1]

    def seg(start, size):
        return _dot(w_ref[start:start + size, :], h)

    y = seg(SEG_QA, 512)
    for hh in range(A_HEADS):
        o1, o2 = _rms_rope_head(y[hh * 64:(hh + 1) * 64], qn_ref[...], cos, sin, HD ** -0.5)
        qa_ref[hh * 64:hh * 64 + 32, :] = o1.astype(qa_ref.dtype)
        qa_ref[hh * 64 + 32:(hh + 1) * 64, :] = o2.astype(qa_ref.dtype)
    y = seg(SEG_KA, 256)
    parts = []
    for hh in range(A_KV_HEADS):
        o1, o2 = _rms_rope_head(y[hh * 64:(hh + 1) * 64], kn_ref[...], cos, sin, 1.0)
        parts += [o1, o2]
    ka_ref[...] = jnp.concatenate(parts, axis=0).T
    va_t = y[128:256]
    va_ref[...] = va_t.T
    vat_ref[...] = va_t.astype(vat_ref.dtype)
    y = seg(SEG_QI, 512)
    for hh in range(IDX_HEADS):
        o1, o2 = _rms_rope_head(y[hh * 64:(hh + 1) * 64], None, cos, sin, IDX_DIM ** -0.5)
        qi_ref[hh * 64:hh * 64 + 32, :] = o1.astype(qi_ref.dtype)
        qi_ref[hh * 64 + 32:(hh + 1) * 64, :] = o2.astype(qi_ref.dtype)
    y = seg(SEG_KIWI, 128)
    o1, o2 = _rms_rope_head(y[0:64], ikn_ref[...], cos, sin, 1.0)
    ki_t = jnp.concatenate([o1, o2, jnp.zeros((64, tm), jnp.float32)], axis=0)
    ki_ref[...] = ki_t.T[:, :64]
    wi_ref[...] = y[64:72] * (IDX_HEADS ** -0.5)
    qb_ref[...] = (seg(SEG_QB, 512) * (HD ** -0.5)).astype(qb_ref.dtype)
    kb_ref[...] = seg(SEG_KB, 512).T
    y = seg(SEG_VB, 512)
    vb_ref[...] = y.T
    vbt_ref[...] = y.astype(vbt_ref.dtype)
    sga_ref[...] = jax.nn.sigmoid(seg(SEG_GA, 1024))
    sgb_ref[...] = jax.nn.sigmoid(seg(SEG_GB, 1024))


def _in_proj(x, n1, w_inT, qn, kn, ikn, cosT, sinT, tm, n_bo, x_token_major):
    n, d = x.shape if x_token_major else x.shape[::-1]
    nt = n // tm
    t_o = n // n_bo
    tpb = t_o // tm
    pos_blocks = cosT.shape[1] // tm
    f32, bf = jnp.float32, MXU_DTYPE
    tok = lambda i: (0, i)
    row = lambda i: (i, 0)
    const = lambda i: (0, 0)
    slab = lambda i: (i // tpb, 0, i % tpb)
    fm = lambda f, dt: jax.ShapeDtypeStruct((n_bo, f, t_o), dt)
    out_shape = (
        fm(512, bf), fm(512, bf), fm(8, f32), fm(512, bf), fm(128, bf), fm(512, bf),
        jax.ShapeDtypeStruct((1024, n), f32),
        jax.ShapeDtypeStruct((1024, n), f32),
        jax.ShapeDtypeStruct((n, 128), f32),
        jax.ShapeDtypeStruct((n, 128), f32),
        jax.ShapeDtypeStruct((n, 64), f32),
        jax.ShapeDtypeStruct((n, 512), f32),
        jax.ShapeDtypeStruct((n, 512), f32),
    )
    out_specs = (
        pl.BlockSpec((None, 512, tm), slab), pl.BlockSpec((None, 512, tm), slab), pl.BlockSpec((None, 8, tm), slab),
        pl.BlockSpec((None, 512, tm), slab), pl.BlockSpec((None, 128, tm), slab), pl.BlockSpec((None, 512, tm), slab),
        pl.BlockSpec((1024, tm), tok), pl.BlockSpec((1024, tm), tok),
        pl.BlockSpec((tm, 128), row), pl.BlockSpec((tm, 128), row), pl.BlockSpec((tm, 64), row),
        pl.BlockSpec((tm, 512), row), pl.BlockSpec((tm, 512), row),
    )
    in_specs = [
        pl.BlockSpec((tm, d), row) if x_token_major else pl.BlockSpec((d, tm), tok),
        pl.BlockSpec((d, 1), const), pl.BlockSpec((N_IN_PAD, d), const),
        pl.BlockSpec((64, 1), const), pl.BlockSpec((64, 1), const), pl.BlockSpec((64, 1), const),
        pl.BlockSpec((32, tm), lambda i: (0, i % pos_blocks)),
        pl.BlockSpec((32, tm), lambda i: (0, i % pos_blocks)),
    ]
    return pl.pallas_call(
        functools.partial(_in_proj_kernel, x_token_major=x_token_major),
        grid=(nt,), in_specs=in_specs, out_specs=out_specs, out_shape=out_shape,
        compiler_params=_cparams(("parallel",)), name="in_proj",
    )(x, n1, w_inT, qn, kn, ikn, cosT, sinT)


def _extent_classes(t_q, s_total):
    if t_q % LANES:
        return [(0, 1, s_total)]
    nq = t_q // LANES
    cq = max(1, nq // N_EXTENT_CLASSES)
    assert nq % cq == 0
    return [(c * cq, cq, (c + 1) * cq * LANES) for c in range(nq // cq)]


def _query_positions(b, qblk, *, t_q, q_pos0, q_blk0):
    lane = lax.broadcasted_iota(jnp.int32, (1, LANES), 1)
    if t_q % LANES == 0:
        return q_pos0 + (q_blk0 + qblk) * LANES + lane, None
    return q_pos0 + lane % t_q, (lane // t_q) == b


def _store_queries(o_ref, rows, val, valid, b):
    if valid is None:
        o_ref[rows, :] = val
    else:
        @pl.when(b == 0)
        def _():
            o_ref[rows, :] = jnp.zeros(val.shape, o_ref.dtype)
        o_ref[rows, :] = jnp.where(valid, val, o_ref[rows, :])


def _pad_rows(x):
    return jnp.concatenate([x, jnp.zeros((LANES - x.shape[0], x.shape[1]), x.dtype)], axis=0)


def _keys_token_major(cache_ref, new_ref):
    if new_ref is None:
        return cache_ref[...]
    return jnp.concatenate([cache_ref[...], _pad_rows(new_ref[...])], axis=0)


def _values_feature_major(cache_t_ref, new_ref):
    if new_ref is None:
        return cache_t_ref[...]
    return jnp.concatenate([cache_t_ref[...], _pad_rows(new_ref[...]).T.astype(MXU_DTYPE)], axis=1)


def _float_order_key(x):
    bits = lax.bitcast_convert_type(x, jnp.int32)
    key = jnp.where(bits < 0, bits ^ jnp.int32(0x7FFFFFFF), bits)
    return jnp.where(x == 0.0, jnp.int32(0), key)


def _count(mask):
    return _col_reduce(jnp.where(mask, 1.0, 0.0), jnp.sum)


def _dsa_kernel(qa_ref, qi_ref, wi_ref, k_ref, vt_ref, ki_ref, *rest, t_q, q_pos0, q_blk0, s_valid, topk):
    b, qblk = pl.program_id(0), pl.program_id(1)
    (kn_ref, vn_ref, kin_ref), o_ref = (rest[:3] if len(rest) == 4 else (None, None, None)), rest[-1]
    k_all = _keys_token_major(k_ref, kn_ref)
    ki_all = _keys_token_major(ki_ref, kin_ref)
    vt_all = _values_feature_major(vt_ref, vn_ref)
    s_pad = k_all.shape[0]
    qpos, valid = _query_positions(b, qblk, t_q=t_q, q_pos0=q_pos0, q_blk0=q_blk0)
    kpos = lax.broadcasted_iota(jnp.int32, (s_pad, LANES), 0)
    adm = ((kpos // CHUNK) <= (qpos // CHUNK)) & (kpos < s_valid)

    qcat = jnp.concatenate([qi_ref[hh * 64:(hh + 1) * 64, :] for hh in range(IDX_HEADS)], axis=1)
    parts = []
    for r0 in range(0, s_pad, KEY_CHUNK):
        r1 = min(s_pad, r0 + KEY_CHUNK)
        z = _dot(ki_all[r0:r1, :].astype(MXU_DTYPE), qcat)
        acc = jnp.maximum(z[:, :LANES], 0.0) * wi_ref[0:1, :]
        for hh in range(1, IDX_HEADS):
            acc = acc + jnp.maximum(z[:, hh * LANES:(hh + 1) * LANES], 0.0) * wi_ref[hh:hh + 1, :]
        parts.append(acc)
    isc = jnp.concatenate(parts, axis=0) if len(parts) > 1 else parts[0]
    key = _float_order_key(jnp.where(adm, isc, NEG))

    kf = float(topk)
    int_min = jnp.int32(-2 ** 31)
    base = jnp.where(_count(key >= 0) >= kf, jnp.int32(0), int_min)

    def bit_step(i, base):
        cand = base + jnp.left_shift(jnp.int32(1), jnp.int32(30) - i)
        return jnp.where(_count(key >= cand) >= kf, cand, base)

    thr = lax.fori_loop(0, 31, bit_step, base)
    gt = key > thr
    tied = key == thr
    need = kf - _count(gt)
    n_tied = _count(tied)
    nbits = int(math.ceil(math.log2(s_pad))) + 1

    def tie_break(_):
        def idx_step(i, lo):
            cand = lo + jnp.left_shift(jnp.int32(1), jnp.int32(nbits - 1) - i)
            return jnp.where(_count(tied & (kpos <= cand)) < need, cand, lo)
        lo = lax.fori_loop(0, nbits, idx_step, jnp.full((1, LANES), -1, jnp.int32))
        return jnp.where(gt | (tied & (kpos <= lo + 1)), 1, 0)

    sel = lax.cond(jnp.max(n_tied - need) > 0.0, tie_break, lambda _: jnp.where(gt | tied, 1, 0), None)
    bias = jnp.where((sel > 0) & adm, 0.0, NEG)

    kb16 = k_all.astype(MXU_DTYPE)
    zeros64 = jnp.zeros((64, LANES), MXU_DTYPE)
    for n in range(A_KV_HEADS):
        qs = []
        for g in range(A_GROUP):
            hh = n * A_GROUP + g
            qh = qa_ref[hh * 64:(hh + 1) * 64, :]
            qs.append(jnp.concatenate([qh, zeros64] if n == 0 else [zeros64, qh], axis=0))
        s = _dot(kb16, jnp.concatenate(qs, axis=1))
        ps, ls = [], []
        for g in range(A_GROUP):
            sg = s[:, g * LANES:(g + 1) * LANES] + bias
            p = jnp.exp(sg - _col_reduce(sg, jnp.max))
            ls.append(_col_reduce(p, jnp.sum))
            ps.append(p.astype(MXU_DTYPE))
        o = _dot(vt_all[n * 64:(n + 1) * 64, :], jnp.concatenate(ps, axis=1))
        for g in range(A_GROUP):
            hh = n * A_GROUP + g
            og = (o[:, g * LANES:(g + 1) * LANES] / ls[g]).astype(o_ref.dtype)
            _store_queries(o_ref, slice(hh * 64, (hh + 1) * 64), og, valid, b)


def _dsa(qaT, qiT, wiT, k, vT, ki, new, *, n_b, t_q, q_pos0, s_valid, topk):
    shared = t_q % LANES != 0
    s_cache = k.shape[1]
    outs = []
    for q_blk0, n_qblk, s_used in _extent_classes(t_q, s_cache):
        qmap = lambda b, q, q0=q_blk0: (0 if shared else b, 0, q0 + q)
        omap = lambda b, q: (0 if shared else b, 0, q)
        kern = functools.partial(_dsa_kernel, t_q=t_q, q_pos0=q_pos0, q_blk0=q_blk0, s_valid=s_valid, topk=topk)
        in_specs = [
            pl.BlockSpec((None, 512, LANES), qmap), pl.BlockSpec((None, 512, LANES), qmap),
            pl.BlockSpec((None, 8, LANES), qmap),
            pl.BlockSpec((None, s_used, 128), lambda b, q: (b, 0, 0)),
            pl.BlockSpec((None, 128, s_used), lambda b, q: (b, 0, 0)),
            pl.BlockSpec((None, s_used, 64), lambda b, q: (b, 0, 0)),
        ]
        args = [qaT, qiT, wiT, k, vT, ki]
        if new is not None:
            in_specs += [pl.BlockSpec((t_q, f), lambda b, q: (b, 0)) for f in (128, 128, 64)]
            args += list(new)
        outs.append(pl.pallas_call(
            kern, grid=(n_b, n_qblk), in_specs=in_specs,
            out_specs=pl.BlockSpec((None, 512, LANES), omap),
            out_shape=jax.ShapeDtypeStruct((qaT.shape[0], 512, n_qblk * LANES), MXU_DTYPE),
            compiler_params=_cparams(("arbitrary", "arbitrary")), name="dsa_attn",
        )(*args))
    return outs[0] if len(outs) == 1 else jnp.concatenate(outs, axis=2)


def _split3(x):
    top_bits = jnp.int32(-65536)
    trunc = lambda v: lax.bitcast_convert_type(lax.bitcast_convert_type(v, jnp.int32) & top_bits, jnp.float32)
    hi = trunc(x)
    r1 = x - hi
    mid = trunc(r1)
    return hi.astype(MXU_DTYPE), mid.astype(MXU_DTYPE), (r1 - mid).astype(MXU_DTYPE)


def _sb_kernel(q_ref, k_ref, vt_ref, *rest, t_q, q_pos0, q_blk0):
    b, qblk = pl.program_id(1), pl.program_id(2)
    (kn_ref, vn_ref), o_ref = (rest[:2] if len(rest) == 3 else (None, None)), rest[-1]
    k16 = _keys_token_major(k_ref, kn_ref).astype(MXU_DTYPE)
    vt_all = _values_feature_major(vt_ref, vn_ref)
    n_blk = k16.shape[0] // LANES
    qpos, valid = _query_positions(b, qblk, t_q=t_q, q_pos0=q_pos0, q_blk0=q_blk0)
    first_qpos = q_pos0 + (q_blk0 * LANES if valid is None else 0)
    row = lax.broadcasted_iota(jnp.int32, (LANES, LANES), 0)
    col = lax.broadcasted_iota(jnp.int32, (LANES, LANES), 1)
    tri = jnp.where(col > row, 1.0, 0.0).astype(MXU_DTYPE)
    zeros64 = jnp.zeros((64, LANES), MXU_DTYPE)
    for hh in range(2):
        qh = q_ref[hh * 64:(hh + 1) * 64, :]
        z_all = _dot(k16, jnp.concatenate([qh, zeros64] if hh == 0 else [zeros64, qh], axis=0))
        blocks, tots = [], []
        for c in range(n_blk):
            z = z_all[c * LANES:(c + 1) * LANES]
            before = None if (c + 1) * LANES <= first_qpos else (c * LANES + row) < qpos
            ls = jnp.minimum(z, 0.0) - jnp.log(1.0 + jnp.exp(-jnp.abs(z)))
            lr = ls - z
            if before is not None:
                lr = jnp.where(before, lr, 0.0)
            aft = _dot(tri, jnp.concatenate(_split3(lr), axis=1))
            aft = aft[:, :LANES] + aft[:, LANES:2 * LANES] + aft[:, 2 * LANES:]
            blocks.append((ls + aft, before))
            tots.append(jnp.sum(lr, axis=0, keepdims=True))
        rest_lr = jnp.zeros((1, LANES), jnp.float32)
        atts = [None] * n_blk
        for c in reversed(range(n_blk)):
            arg, before = blocks[c]
            att = jnp.exp(arg + rest_lr)
            if before is not None:
                att = jnp.where(before, att, 0.0)
            atts[c] = att.astype(MXU_DTYPE)
            rest_lr = rest_lr + tots[c]
        att = jnp.concatenate(atts, axis=0) if n_blk > 1 else atts[0]
        og = _dot(vt_all[hh * 64:(hh + 1) * 64, :], att).astype(o_ref.dtype)
        _store_queries(o_ref, slice(hh * 64, (hh + 1) * 64), og, valid, b)


def _sb(qbT, kb, vbT, new, *, n_b, t_q, q_pos0):
    shared = t_q % LANES != 0
    outs = []
    for q_blk0, n_qblk, s_used in _extent_classes(t_q, kb.shape[1]):
        qmap = lambda hp, b, q, q0=q_blk0: (0 if shared else b, hp, q0 + q)
        omap = lambda hp, b, q: (0 if shared else b, hp, q)
        kern = functools.partial(_sb_kernel, t_q=t_q, q_pos0=q_pos0, q_blk0=q_blk0)
        in_specs = [
            pl.BlockSpec((None, 128, LANES), qmap),
            pl.BlockSpec((None, s_used, 128), lambda hp, b, q: (b, 0, hp)),
            pl.BlockSpec((None, 128, s_used), lambda hp, b, q: (b, hp, 0)),
        ]
        args = [qbT, kb, vbT]
        if new is not None:
            in_specs += [pl.BlockSpec((t_q, 128), lambda hp, b, q: (b, hp))] * 2
            args += list(new)
        outs.append(pl.pallas_call(
            kern, grid=(B_HEADS // 2, n_b, n_qblk), in_specs=in_specs,
            out_specs=pl.BlockSpec((None, 128, LANES), omap),
            out_shape=jax.ShapeDtypeStruct((qbT.shape[0], 512, n_qblk * LANES), MXU_DTYPE),
            compiler_params=_cparams(("arbitrary", "arbitrary", "arbitrary")), name="sb_attn",
        )(*args))
    return outs[0] if len(outs) == 1 else jnp.concatenate(outs, axis=2)


def _top_values(s, n):
    rows = s.shape[0]
    rid = lax.broadcasted_iota(jnp.int32, s.shape, 0).astype(jnp.float32)
    vals = []
    for _ in range(n):
        m = jnp.max(s, axis=0, keepdims=True)
        first = jnp.min(jnp.where(s == m, rid, float(rows)), axis=0, keepdims=True)
        vals.append(m)
        s = jnp.where(rid == first, -jnp.inf, s)
    return vals


def _candidate_sums(a1, a2):
    tm = a1[0].shape[1]
    ninf = jnp.full((1, tm), -jnp.inf, jnp.float32)
    pad = [ninf] * (3 * SUBLANES - (P_TOPK + 1))
    a1m = jnp.concatenate(a1 + pad, axis=0)
    a2m = jnp.concatenate(a2 + pad, axis=0)
    r8 = lax.broadcasted_iota(jnp.int32, (SUBLANES, tm), 0)
    pieces = [a1[0] + a2m, a1[1] + a2m[0:SUBLANES]]
    for i in range(2, SUBLANES):
        pieces.append(jnp.where(r8 < (P_TOPK + 1) // (i + 1), a1[i] + a2m[0:SUBLANES], -jnp.inf))
    pieces.append(a1m[SUBLANES:] + a2[0])
    return jnp.concatenate(pieces, axis=0)


def _out_proj_kernel(x_ref, oa_ref, ob_ref, sga_ref, sgb_ref, wpa_ref, wpb_ref, wo_ref, n2_ref,
                     wpq_ref, k1_ref, k2_ref,
                     x1_ref, h2_ref, e1_ref, e2_ref, gthr_ref, q_scr, *, x_token_major):
    m = sga_ref[...] * _dot(wpa_ref[...], oa_ref[...]) + sgb_ref[...] * _dot(wpb_ref[...], ob_ref[...])
    x = x_ref[...].T if x_token_major else x_ref[...]
    x1 = x + _dot(wo_ref[...], m.astype(MXU_DTYPE))
    x1_ref[...] = x1
    ms = jnp.mean(x1 * x1, axis=0, keepdims=True)
    h2 = (x1 * lax.rsqrt(ms + EPS) * n2_ref[...]).astype(MXU_DTYPE)
    h2_ref[...] = h2
    q_scr[...] = _dot(wpq_ref[...], h2).astype(MXU_DTYPE)

    def head(hh):
        r0 = pl.multiple_of(hh * 128, 128)
        s1 = _dot(k1_ref[hh], q_scr[pl.ds(r0, 64), :])
        s2 = _dot(k2_ref[hh], q_scr[pl.ds(pl.multiple_of(r0 + 64, 64), 64), :])
        a1 = _top_values(s1, P_TOPK + 1)
        a2 = _top_values(s2, P_TOPK + 1)
        top = _top_values(_candidate_sums(a1, a2), P_TOPK + 1)
        mx = top[0]
        z = jnp.zeros_like(mx)
        for v in top[:P_TOPK]:
            z = z + jnp.exp(v - mx)
        rz = 1.0 / z
        e1_ref[hh] = jnp.exp(s1 - a1[0]) * rz
        e2_ref[hh] = jnp.exp(s2 - a2[0])
        gthr_ref[pl.ds(hh, 1), :] = jnp.exp(0.5 * (top[P_TOPK - 1] + top[P_TOPK]) - mx) * rz

    def head_pair(i, carry):
        head(2 * i)
        head(2 * i + 1)
        return carry

    lax.fori_loop(0, P_HEADS // 2, head_pair, 0)


def _out_proj(x, oaT, obT, sgaT, sgbT, w_paT, w_pbT, w_oT, n2, w_pqT, k1, k2, tm, x_token_major):
    n, d = x.shape if x_token_major else x.shape[::-1]
    nt = n // tm
    tpb = oaT.shape[2] // tm
    f32, bf = jnp.float32, MXU_DTYPE
    tok = lambda i: (0, i)
    tok3 = lambda i: (0, 0, i)
    slab = lambda i: (i // tpb, 0, i % tpb)
    c2 = lambda i: (0, 0)
    c3 = lambda i: (0, 0, 0)
    out_shape = (
        jax.ShapeDtypeStruct((d, n), f32), jax.ShapeDtypeStruct((d, n), bf),
        jax.ShapeDtypeStruct((P_HEADS, N_KEYS, n), f32), jax.ShapeDtypeStruct((P_HEADS, N_KEYS, n), f32),
        jax.ShapeDtypeStruct((P_HEADS, n), f32),
    )
    out_specs = (
        pl.BlockSpec((d, tm), tok), pl.BlockSpec((d, tm), tok),
        pl.BlockSpec((P_HEADS, N_KEYS, tm), tok3), pl.BlockSpec((P_HEADS, N_KEYS, tm), tok3),
        pl.BlockSpec((P_HEADS, tm), tok),
    )
    in_specs = [
        pl.BlockSpec((tm, d), lambda i: (i, 0)) if x_token_major else pl.BlockSpec((d, tm), tok),
        pl.BlockSpec((None, 512, tm), slab), pl.BlockSpec((None, 512, tm), slab),
        pl.BlockSpec((d, tm), tok), pl.BlockSpec((d, tm), tok),
        pl.BlockSpec((d, 512), c2), pl.BlockSpec((d, 512), c2), pl.BlockSpec((d, d), c2),
        pl.BlockSpec((d, 1), c2), pl.BlockSpec((P_HEADS * 128, d), c2),
        pl.BlockSpec((P_HEADS, N_KEYS, P_HALF), c3), pl.BlockSpec((P_HEADS, N_KEYS, P_HALF), c3),
    ]
    return pl.pallas_call(
        functools.partial(_out_proj_kernel, x_token_major=x_token_major),
        grid=(nt,), in_specs=in_specs, out_specs=out_specs, out_shape=out_shape,
        scratch_shapes=[pltpu.VMEM((P_HEADS * 128, tm), bf)],
        compiler_params=_cparams(("parallel",)), name="out_proj",
    )(x, oaT, obT, sgaT, sgbT, w_paT, w_pbT, w_oT, n2, w_pqT, k1, k2)


def _gelu_exact(x):
    return 0.5 * x * (1.0 + lax.erf(x * np.float32(math.sqrt(0.5))))


def _peer_kernel(h2_ref, u_ref, vt_ref, e1_ref, e2_ref, gthr_ref, x1_ref, o_ref, acc_ref, coef_ref, *,
                 out_token_major):
    ec = pl.program_id(1)
    tm = h2_ref.shape[1]

    @pl.when(ec == 0)
    def _():
        acc_ref[...] = jnp.zeros_like(acc_ref)

    a_all = _dot(u_ref[...], h2_ref[...])
    i1_group = pl.ds(pl.multiple_of(ec * SUBLANES, SUBLANES), SUBLANES)
    for c in range(tm // LANES):
        cs = slice(c * LANES, (c + 1) * LANES)
        e1g = [e1_ref[hh, i1_group, cs] for hh in range(P_HEADS)]
        gth = [gthr_ref[hh:hh + 1, cs] for hh in range(P_HEADS)]
        for il in range(SUBLANES):
            gate = jnp.zeros((N_KEYS, LANES), jnp.float32)
            for hh in range(P_HEADS):
                g = e1g[hh][il:il + 1, :] * e2_ref[hh, :, cs]
                gate = gate + jnp.where(g >= gth[hh], g, 0.0)
            a = a_all[il * N_KEYS:(il + 1) * N_KEYS, cs]
            coef_ref[il * N_KEYS:(il + 1) * N_KEYS, cs] = (gate * _gelu_exact(a)).astype(coef_ref.dtype)
    acc_ref[...] += _dot(vt_ref[...], coef_ref[...])

    @pl.when(ec == pl.num_programs(1) - 1)
    def _():
        x2 = x1_ref[...] + acc_ref[...]
        o_ref[...] = x2.T if out_token_major else x2


def _peer(h2T, u, vT, e1T, e2T, gthr, x1T, tm, out_token_major):
    d, n = x1T.shape
    n_exp = u.shape[0]
    tok = lambda i, e: (0, i)
    tok3 = lambda i, e: (0, 0, i)
    in_specs = [
        pl.BlockSpec((d, tm), tok),
        pl.BlockSpec((TE_PEER, d), lambda i, e: (e, 0)),
        pl.BlockSpec((d, TE_PEER), lambda i, e: (0, e)),
        pl.BlockSpec((P_HEADS, N_KEYS, tm), tok3), pl.BlockSpec((P_HEADS, N_KEYS, tm), tok3),
        pl.BlockSpec((P_HEADS, tm), tok),
        pl.BlockSpec((d, tm), tok),
    ]
    return pl.pallas_call(
        functools.partial(_peer_kernel, out_token_major=out_token_major), grid=(n // tm, n_exp // TE_PEER),
        in_specs=in_specs,
        out_specs=pl.BlockSpec((tm, d), lambda i, e: (i, 0)) if out_token_major else pl.BlockSpec((d, tm), tok),
        out_shape=jax.ShapeDtypeStruct((n, d) if out_token_major else (d, n), jnp.float32),
        scratch_shapes=[pltpu.VMEM((d, tm), jnp.float32), pltpu.VMEM((TE_PEER, tm), MXU_DTYPE)],
        compiler_params=_cparams(("parallel", "arbitrary")), name="peer_dense",
    )(h2T, u, vT, e1T, e2T, gthr, x1T)


def _pack_w_in(w_in):
    d = w_in.shape[0]
    cols = [w_in[:, 0:1280], w_in[:, 1280:1352], jnp.zeros((d, 56), w_in.dtype), w_in[:, 1352:]]
    return jnp.concatenate(cols, axis=1).T.astype(MXU_DTYPE)


def _rope_tables(pos):
    half = HD // 2
    inv = ROPE_THETA ** (-jnp.arange(half, dtype=jnp.float32) / half)
    ang = pos.astype(jnp.float32)[None, :] * inv[:, None]
    return jnp.cos(ang), jnp.sin(ang)


def _token_tile(n, pref):
    return pref if n % pref == 0 else LANES


def _layer(x, lw, rope, n_b, t_q, q_pos0, past, topk, x_token_major, out_token_major):
    (n1, w_inT, qn, kn, ikn, w_paT, w_pbT, w_oT, n2, w_pqT, k1, k2, u, vT) = lw
    n = n_b * t_q
    tm = _token_tile(n, TM_PROJ)
    n_bo = n_b if past is None else 1
    (qaT, qiT, wiT, qbT, vaT, vbT, sgaT, sgbT, ka, va, ki, kb, vb) = _in_proj(
        x, n1, w_inT, qn, kn, ikn, rope[0], rope[1], tm, n_bo, x_token_major)
    new = (ka, va, ki, kb, vb)
    if past is None:
        s_valid = t_q
        k_a = ka.reshape(n_b, t_q, 128)
        k_i = ki.reshape(n_b, t_q, 64)
        k_b = kb.reshape(n_b, t_q, 512)
        v_aT, v_bT = vaT, vbT
        new_a = new_b = None
    else:
        k_a, v_aT, k_i, k_b, v_bT = past
        s_valid = k_a.shape[1] + t_q
        new_a, new_b = (ka, va, ki), (kb, vb)
    oaT = _dsa(qaT, qiT, wiT, k_a, v_aT, k_i, new_a, n_b=n_b, t_q=t_q, q_pos0=q_pos0, s_valid=s_valid, topk=topk)
    obT = _sb(qbT, k_b, v_bT, new_b, n_b=n_b, t_q=t_q, q_pos0=q_pos0)
    x1T, h2T, e1T, e2T, gthr = _out_proj(
        x, oaT, obT, sgaT, sgbT, w_paT, w_pbT, w_oT, n2, w_pqT, k1, k2, tm, x_token_major)
    x2 = _peer(h2T, u, vT, e1T, e2T, gthr, x1T, _token_tile(n, TM_PEER), out_token_major)
    return x2, new


def kernel(x_prompt, x_sample, cache_a_k, cache_a_v, cache_idx_k, cache_b_k, cache_b_v, norm1, w_in, q_norm_a, k_norm_a, idx_k_norm, w_pa, w_pb, w_o, norm2, peer_wq, peer_k1, peer_k2, peer_u, peer_v):
    n_bp, t_p, d = x_prompt.shape
    n_bs, t_s, _ = x_sample.shape
    depth = w_in.shape[0]
    past_len = cache_a_k.shape[2]
    assert t_p % LANES == 0 and n_bs * t_s == LANES and past_len % LANES == 0 and t_s % SUBLANES == 0
    assert peer_u.shape[1] % TE_PEER == 0
    topk_p = min(TOPK_MAX, t_p // 4)
    topk_s = min(TOPK_MAX, (past_len + t_s) // 4)
    bf = MXU_DTYPE
    col = lambda g: g.reshape(-1, 1)

    rope_p = _rope_tables(jnp.arange(t_p, dtype=jnp.int32))
    rope_s = _rope_tables(jnp.tile(past_len + jnp.arange(t_s, dtype=jnp.int32), n_bs))

    xp = x_prompt.reshape(n_bp * t_p, d)
    xs = x_sample.reshape(n_bs * t_s, d)
    st_p, st_s = [], []
    for l in range(depth):
        first, last = l == 0, l == depth - 1
        lw = (col(norm1[l]), _pack_w_in(w_in[l]), col(q_norm_a[l]), col(k_norm_a[l]), col(idx_k_norm[l]),
              w_pa[l].T.astype(bf), w_pb[l].T.astype(bf), w_o[l].T.astype(bf), col(norm2[l]),
              peer_wq[l].T.astype(bf), peer_k1[l].astype(bf), peer_k2[l].astype(bf),
              peer_u[l].astype(bf), peer_v[l].T.astype(bf))
        past = (cache_a_k[l].reshape(n_bs, past_len, -1),
                cache_a_v[l].reshape(n_bs, past_len, -1).transpose(0, 2, 1).astype(bf),
                cache_idx_k[l],
                cache_b_k[l].reshape(n_bs, past_len, -1),
                cache_b_v[l].reshape(n_bs, past_len, -1).transpose(0, 2, 1).astype(bf))
        xp, sp = _layer(xp, lw, rope_p, n_bp, t_p, 0, None, topk_p, first, last)
        xs, ss = _layer(xs, lw, rope_s, n_bs, t_s, past_len, past, topk_s, first, last)
        st_p.append(sp)
        st_s.append(ss)

    def stack(sts, j, n_b, t, shape):
        return jnp.stack([s[j] for s in sts]).reshape((depth, n_b, t) + shape)

    outs = [xp.reshape(n_bp, t_p, d), xs.reshape(n_bs, t_s, d)]
    shapes = ((A_KV_HEADS, HD), (A_KV_HEADS, HD), (IDX_DIM,), (B_HEADS, HD), (B_HEADS, HD))
    for sts, n_b, t in ((st_p, n_bp, t_p), (st_s, n_bs, t_s)):
        for j, shp in enumerate(shapes):
            outs.append(stack(sts, j, n_b, t, shp))
    return tuple(outs)
```

```python
import functools
import math

import jax
import jax.numpy as jnp
import numpy as np
from jax import lax
from jax.experimental import pallas as pl
from jax.experimental.pallas import tpu as pltpu

CHUNK = 64
EPS = 1e-6
ROPE_THETA = 10000.0
NEG = -1e30
A_HEADS, A_KV_HEADS, HD = 8, 2, 64
A_GROUP = A_HEADS // A_KV_HEADS
IDX_HEADS, IDX_DIM = 8, 64
TOPK_MAX = 256
B_HEADS = 8
P_HEADS, N_KEYS, P_HALF, P_TOPK = 8, 128, 64, 16

LANES = 128
SUBLANES = 8
VMEM_LIMIT = 56 * 1024 * 1024
TM_PROJ = 256
TM_PEER = 512
TE_PEER = SUBLANES * N_KEYS
KEY_CHUNK = 512
N_EXTENT_CLASSES = 4

MXU_DTYPE = jnp.bfloat16

SEG_QA, SEG_KA, SEG_VA, SEG_QI, SEG_KIWI = 0, 512, 640, 768, 1280
SEG_QB, SEG_KB, SEG_VB, SEG_GA, SEG_GB, N_IN_PAD = 1408, 1920, 2432, 2944, 3968, 4992


def _cparams(sem):
    return pltpu.CompilerParams(dimension_semantics=sem, vmem_limit_bytes=VMEM_LIMIT)


def _dot(a, b):
    return jnp.dot(a, b, preferred_element_type=jnp.float32)


def _col_reduce(x, op):
    rows = x.shape[0]
    if rows % 64 == 0 and rows > 64:
        x = op(x.reshape(rows // 64, 64, x.shape[1]), axis=0)
    return op(x, axis=0, keepdims=True)


def _rms_rope_head(blk, gain, cos, sin, scale):
    if gain is not None:
        ms = jnp.mean(blk * blk, axis=0, keepdims=True)
        blk = blk * lax.rsqrt(ms + EPS) * gain
    x1, x2 = blk[:32], blk[32:]
    o1 = x1 * cos - x2 * sin
    o2 = x2 * cos + x1 * sin
    if scale != 1.0:
        o1, o2 = o1 * scale, o2 * scale
    return o1, o2


def _in_proj_kernel(x_ref, n1_ref, w_ref, qn_ref, kn_ref, ikn_ref, cos_ref, sin_ref,
                    qa_ref, qi_ref, wi_ref, qb_ref, vat_ref, vbt_ref, sga_ref, sgb_ref,
                    ka_ref, va_ref, ki_ref, kb_ref, vb_ref, *, x_token_major):
    x = x_ref[...].T if x_token_major else x_ref[...]
    ms = jnp.mean(x * x, axis=0, keepdims=True)
    h = (x * lax.rsqrt(ms + EPS) * n1_ref[...]).astype(MXU_DTYPE)
    cos, sin = cos_ref[...], sin_ref[...]
    tm = x.shape[1]

    def seg(start, size):
        return _dot(w_ref[start:start + size, :], h)

    y = seg(SEG_QA, 512)
    for hh in range(A_HEADS):
        o1, o2 = _rms_rope_head(y[hh * 64:(hh + 1) * 64], qn_ref[...], cos, sin, HD ** -0.5)
        qa_ref[hh * 64:hh * 64 + 32, :] = o1.astype(qa_ref.dtype)
        qa_ref[hh * 64 + 32:(hh + 1) * 64, :] = o2.astype(qa_ref.dtype)
    y = seg(SEG_KA, 256)
    parts = []
    for hh in range(A_KV_HEADS):
        o1, o2 = _rms_rope_head(y[hh * 64:(hh + 1) * 64], kn_ref[...], cos, sin, 1.0)
        parts += [o1, o2]
    ka_ref[...] = jnp.concatenate(parts, axis=0).T
    va_t = y[128:256]
    va_ref[...] = va_t.T
    vat_ref[...] = va_t.astype(vat_ref.dtype)
    y = seg(SEG_QI, 512)
    for hh in range(IDX_HEADS):
        o1, o2 = _rms_rope_head(y[hh * 64:(hh + 1) * 64], None, cos, sin, IDX_DIM ** -0.5)
        qi_ref[hh * 64:hh * 64 + 32, :] = o1.astype(qi_ref.dtype)
        qi_ref[hh * 64 + 32:(hh + 1) * 64, :] = o2.astype(qi_ref.dtype)
    y = seg(SEG_KIWI, 128)
    o1, o2 = _rms_rope_head(y[0:64], ikn_ref[...], cos, sin, 1.0)
    ki_t = jnp.concatenate([o1, o2, jnp.zeros((64, tm), jnp.float32)], axis=0)
    ki_ref[...] = ki_t.T[:, :64]
    wi_ref[...] = y[64:72] * (IDX_HEADS ** -0.5)
    qb_ref[...] = (seg(SEG_QB, 512) * (HD ** -0.5)).astype(qb_ref.dtype)
    kb_ref[...] = seg(SEG_KB, 512).T
    y = seg(SEG_VB, 512)
    vb_ref[...] = y.T
    vbt_ref[...] = y.astype(vbt_ref.dtype)
    sga_ref[...] = jax.nn.sigmoid(seg(SEG_GA, 1024))
    sgb_ref[...] = jax.nn.sigmoid(seg(SEG_GB, 1024))


def _in_proj(x, n1, w_inT, qn, kn, ikn, cosT, sinT, tm, n_bo, x_token_major):
    n, d = x.shape if x_token_major else x.shape[::-1]
    nt = n // tm
    t_o = n // n_bo
    tpb = t_o // tm
    pos_blocks = cosT.shape[1] // tm
    f32, bf = jnp.float32, MXU_DTYPE
    tok = lambda i: (0, i)
    row = lambda i: (i, 0)
    const = lambda i: (0, 0)
    slab = lambda i: (i // tpb, 0, i % tpb)
    fm = lambda f, dt: jax.ShapeDtypeStruct((n_bo, f, t_o), dt)
    out_shape = (
        fm(512, bf), fm(512, bf), fm(8, f32), fm(512, bf), fm(128, bf), fm(512, bf),
        jax.ShapeDtypeStruct((1024, n), f32),
        jax.ShapeDtypeStruct((1024, n), f32),
        jax.ShapeDtypeStruct((n, 128), f32),
        jax.ShapeDtypeStruct((n, 128), f32),
        jax.ShapeDtypeStruct((n, 64), f32),
        jax.ShapeDtypeStruct((n, 512), f32),
        jax.ShapeDtypeStruct((n, 512), f32),
    )
    out_specs = (
        pl.BlockSpec((None, 512, tm), slab), pl.BlockSpec((None, 512, tm), slab), pl.BlockSpec((None, 8, tm), slab),
        pl.BlockSpec((None, 512, tm), slab), pl.BlockSpec((None, 128, tm), slab), pl.BlockSpec((None, 512, tm), slab),
        pl.BlockSpec((1024, tm), tok), pl.BlockSpec((1024, tm), tok),
        pl.BlockSpec((tm, 128), row), pl.BlockSpec((tm, 128), row), pl.BlockSpec((tm, 64), row),
        pl.BlockSpec((tm, 512), row), pl.BlockSpec((tm, 512), row),
    )
    in_specs = [
        pl.BlockSpec((tm, d), row) if x_token_major else pl.BlockSpec((d, tm), tok),
        pl.BlockSpec((d, 1), const), pl.BlockSpec((N_IN_PAD, d), const),
        pl.BlockSpec((64, 1), const), pl.BlockSpec((64, 1), const), pl.BlockSpec((64, 1), const),
        pl.BlockSpec((32, tm), lambda i: (0, i % pos_blocks)),
        pl.BlockSpec((32, tm), lambda i: (0, i % pos_blocks)),
    ]
    return pl.pallas_call(
        functools.partial(_in_proj_kernel, x_token_major=x_token_major),
        grid=(nt,), in_specs=in_specs, out_specs=out_specs, out_shape=out_shape,
        compiler_params=_cparams(("parallel",)), name="in_proj",
    )(x, n1, w_inT, qn, kn, ikn, cosT, sinT)


def _extent_classes(t_q, s_total):
    if t_q % LANES:
        return [(0, 1, s_total)]
    nq = t_q // LANES
    cq = max(1, nq // N_EXTENT_CLASSES)
    assert nq % cq == 0
    return [(c * cq, cq, (c + 1) * cq * LANES) for c in range(nq // cq)]


def _query_positions(b, qblk, *, t_q, q_pos0, q_blk0):
    lane = lax.broadcasted_iota(jnp.int32, (1, LANES), 1)
    if t_q % LANES == 0:
        return q_pos0 + (q_blk0 + qblk) * LANES + lane, None
    return q_pos0 + lane % t_q, (lane // t_q) == b


def _store_queries(o_ref, rows, val, valid, b):
    if valid is None:
        o_ref[rows, :] = val
    else:
        @pl.when(b == 0)
        def _():
            o_ref[rows, :] = jnp.zeros(val.shape, o_ref.dtype)
        o_ref[rows, :] = jnp.where(valid, val, o_ref[rows, :])


def _pad_rows(x):
    return jnp.concatenate([x, jnp.zeros((LANES - x.shape[0], x.shape[1]), x.dtype)], axis=0)


def _keys_token_major(cache_ref, new_ref):
    if new_ref is None:
        return cache_ref[...]
    return jnp.concatenate([cache_ref[...], _pad_rows(new_ref[...])], axis=0)


def _values_feature_major(cache_t_ref, new_ref):
    if new_ref is None:
        return cache_t_ref[...]
    return jnp.concatenate([cache_t_ref[...], _pad_rows(new_ref[...]).T.astype(MXU_DTYPE)], axis=1)


def _float_order_key(x):
    bits = lax.bitcast_convert_type(x, jnp.int32)
    key = jnp.where(bits < 0, bits ^ jnp.int32(0x7FFFFFFF), bits)
    return jnp.where(x == 0.0, jnp.int32(0), key)


def _count(mask):
    return _col_reduce(jnp.where(mask, 1.0, 0.0), jnp.sum)


def _dsa_kernel(qa_ref, qi_ref, wi_ref, k_ref, vt_ref, ki_ref, *rest, t_q, q_pos0, q_blk0, s_valid, topk):
    b, qblk = pl.program_id(0), pl.program_id(1)
    (kn_ref, vn_ref, kin_ref), o_ref = (rest[:3] if len(rest) == 4 else (None, None, None)), rest[-1]
    k_all = _keys_token_major(k_ref, kn_ref)
    ki_all = _keys_token_major(ki_ref, kin_ref)
    vt_all = _values_feature_major(vt_ref, vn_ref)
    s_pad = k_all.shape[0]
    qpos, valid = _query_positions(b, qblk, t_q=t_q, q_pos0=q_pos0, q_blk0=q_blk0)
    kpos = lax.broadcasted_iota(jnp.int32, (s_pad, LANES), 0)
    adm = ((kpos // CHUNK) <= (qpos // CHUNK)) & (kpos < s_valid)

    qcat = jnp.concatenate([qi_ref[hh * 64:(hh + 1) * 64, :] for hh in range(IDX_HEADS)], axis=1)
    parts = []
    for r0 in range(0, s_pad, KEY_CHUNK):
        r1 = min(s_pad, r0 + KEY_CHUNK)
        z = _dot(ki_all[r0:r1, :].astype(MXU_DTYPE), qcat)
        acc = jnp.maximum(z[:, :LANES], 0.0) * wi_ref[0:1, :]
        for hh in range(1, IDX_HEADS):
            acc = acc + jnp.maximum(z[:, hh * LANES:(hh + 1) * LANES], 0.0) * wi_ref[hh:hh + 1, :]
        parts.append(acc)
    isc = jnp.concatenate(parts, axis=0) if len(parts) > 1 else parts[0]
    key = _float_order_key(jnp.where(adm, isc, NEG))

    kf = float(topk)
    int_min = jnp.int32(-2 ** 31)
    base = jnp.where(_count(key >= 0) >= kf, jnp.int32(0), int_min)

    def bit_step(i, base):
        cand = base + jnp.left_shift(jnp.int32(1), jnp.int32(30) - i)
        return jnp.where(_count(key >= cand) >= kf, cand, base)

    thr = lax.fori_loop(0, 31, bit_step, base)
    gt = key > thr
    tied = key == thr
    need = kf - _count(gt)
    n_tied = _count(tied)
    nbits = int(math.ceil(math.log2(s_pad))) + 1

    def tie_break(_):
        def idx_step(i, lo):
            cand = lo + jnp.left_shift(jnp.int32(1), jnp.int32(nbits - 1) - i)
            return jnp.where(_count(tied & (kpos <= cand)) < need, cand, lo)
        lo = lax.fori_loop(0, nbits, idx_step, jnp.full((1, LANES), -1, jnp.int32))
        return jnp.where(gt | (tied & (kpos <= lo + 1)), 1, 0)

    sel = lax.cond(jnp.max(n_tied - need) > 0.0, tie_break, lambda _: jnp.where(gt | tied, 1, 0), None)
    bias = jnp.where((sel > 0) & adm, 0.0, NEG)

    kb16 = k_all.astype(MXU_DTYPE)
    zeros64 = jnp.zeros((64, LANES), MXU_DTYPE)
    for n in range(A_KV_HEADS):
        qs = []
        for g in range(A_GROUP):
            hh = n * A_GROUP + g
            qh = qa_ref[hh * 64:(hh + 1) * 64, :]
            qs.append(jnp.concatenate([qh, zeros64] if n == 0 else [zeros64, qh], axis=0))
        s = _dot(kb16, jnp.concatenate(qs, axis=1))
        ps, ls = [], []
        for g in range(A_GROUP):
            sg = s[:, g * LANES:(g + 1) * LANES] + bias
            p = jnp.exp(sg - _col_reduce(sg, jnp.max))
            ls.append(_col_reduce(p, jnp.sum))
            ps.append(p.astype(MXU_DTYPE))
        o = _dot(vt_all[n * 64:(n + 1) * 64, :], jnp.concatenate(ps, axis=1))
        for g in range(A_GROUP):
            hh = n * A_GROUP + g
            og = (o[:, g * LANES:(g + 1) * LANES] / ls[g]).astype(o_ref.dtype)
            _store_queries(o_ref, slice(hh * 64, (hh + 1) * 64), og, valid, b)


def _dsa(qaT, qiT, wiT, k, vT, ki, new, *, n_b, t_q, q_pos0, s_valid, topk):
    shared = t_q % LANES != 0
    s_cache = k.shape[1]
    outs = []
    for q_blk0, n_qblk, s_used in _extent_classes(t_q, s_cache):
        qmap = lambda b, q, q0=q_blk0: (0 if shared else b, 0, q0 + q)
        omap = lambda b, q: (0 if shared else b, 0, q)
        kern = functools.partial(_dsa_kernel, t_q=t_q, q_pos0=q_pos0, q_blk0=q_blk0, s_valid=s_valid, topk=topk)
        in_specs = [
            pl.BlockSpec((None, 512, LANES), qmap), pl.BlockSpec((None, 512, LANES), qmap),
            pl.BlockSpec((None, 8, LANES), qmap),
            pl.BlockSpec((None, s_used, 128), lambda b, q: (b, 0, 0)),
            pl.BlockSpec((None, 128, s_used), lambda b, q: (b, 0, 0)),
            pl.BlockSpec((None, s_used, 64), lambda b, q: (b, 0, 0)),
        ]
        args = [qaT, qiT, wiT, k, vT, ki]
        if new is not None:
            in_specs += [pl.BlockSpec((t_q, f), lambda b, q: (b, 0)) for f in (128, 128, 64)]
            args += list(new)
        outs.append(pl.pallas_call(
            kern, grid=(n_b, n_qblk), in_specs=in_specs,
            out_specs=pl.BlockSpec((None, 512, LANES), omap),
            out_shape=jax.ShapeDtypeStruct((qaT.shape[0], 512, n_qblk * LANES), MXU_DTYPE),
            compiler_params=_cparams(("arbitrary", "arbitrary")), name="dsa_attn",
        )(*args))
    return outs[0] if len(outs) == 1 else jnp.concatenate(outs, axis=2)


def _split3(x):
    top_bits = jnp.int32(-65536)
    trunc = lambda v: lax.bitcast_convert_type(lax.bitcast_convert_type(v, jnp.int32) & top_bits, jnp.float32)
    hi = trunc(x)
    r1 = x - hi
    mid = trunc(r1)
    return hi.astype(MXU_DTYPE), mid.astype(MXU_DTYPE), (r1 - mid).astype(MXU_DTYPE)


def _sb_kernel(q_ref, k_ref, vt_ref, *rest, t_q, q_pos0, q_blk0):
    b, qblk = pl.program_id(1), pl.program_id(2)
    (kn_ref, vn_ref), o_ref = (rest[:2] if len(rest) == 3 else (None, None)), rest[-1]
    k16 = _keys_token_major(k_ref, kn_ref).astype(MXU_DTYPE)
    vt_all = _values_feature_major(vt_ref, vn_ref)
    n_blk = k16.shape[0] // LANES
    qpos, valid = _query_positions(b, qblk, t_q=t_q, q_pos0=q_pos0, q_blk0=q_blk0)
    first_qpos = q_pos0 + (q_blk0 * LANES if valid is None else 0)
    row = lax.broadcasted_iota(jnp.int32, (LANES, LANES), 0)
    col = lax.broadcasted_iota(jnp.int32, (LANES, LANES), 1)
    tri = jnp.where(col > row, 1.0, 0.0).astype(MXU_DTYPE)
    zeros64 = jnp.zeros((64, LANES), MXU_DTYPE)
    for hh in range(2):
        qh = q_ref[hh * 64:(hh + 1) * 64, :]
        z_all = _dot(k16, jnp.concatenate([qh, zeros64] if hh == 0 else [zeros64, qh], axis=0))
        blocks, tots = [], []
        for c in range(n_blk):
            z = z_all[c * LANES:(c + 1) * LANES]
            before = None if (c + 1) * LANES <= first_qpos else (c * LANES + row) < qpos
            ls = jnp.minimum(z, 0.0) - jnp.log(1.0 + jnp.exp(-jnp.abs(z)))
            lr = ls - z
            if before is not None:
                lr = jnp.where(before, lr, 0.0)
            aft = _dot(tri, jnp.concatenate(_split3(lr), axis=1))
            aft = aft[:, :LANES] + aft[:, LANES:2 * LANES] + aft[:, 2 * LANES:]
            blocks.append((ls + aft, before))
            tots.append(jnp.sum(lr, axis=0, keepdims=True))
        rest_lr = jnp.zeros((1, LANES), jnp.float32)
        atts = [None] * n_blk
        for c in reversed(range(n_blk)):
            arg, before = blocks[c]
            att = jnp.exp(arg + rest_lr)
            if before is not None:
                att = jnp.where(before, att, 0.0)
            atts[c] = att.astype(MXU_DTYPE)
            rest_lr = rest_lr + tots[c]
        att = jnp.concatenate(atts, axis=0) if n_blk > 1 else atts[0]
        og = _dot(vt_all[hh * 64:(hh + 1) * 64, :], att).astype(o_ref.dtype)
        _store_queries(o_ref, slice(hh * 64, (hh + 1) * 64), og, valid, b)


def _sb(qbT, kb, vbT, new, *, n_b, t_q, q_pos0):
    shared = t_q % LANES != 0
    outs = []
    for q_blk0, n_qblk, s_used in _extent_classes(t_q, kb.shape[1]):
        qmap = lambda hp, b, q, q0=q_blk0: (0 if shared else b, hp, q0 + q)
        omap = lambda hp, b, q: (0 if shared else b, hp, q)
        kern = functools.partial(_sb_kernel, t_q=t_q, q_pos0=q_pos0, q_blk0=q_blk0)
        in_specs = [
            pl.BlockSpec((None, 128, LANES), qmap),
            pl.BlockSpec((None, s_used, 128), lambda hp, b, q: (b, 0, hp)),
            pl.BlockSpec((None, 128, s_used), lambda hp, b, q: (b, hp, 0)),
        ]
        args = [qbT, kb, vbT]
        if new is not None:
            in_specs += [pl.BlockSpec((t_q, 128), lambda hp, b, q: (b, hp))] * 2
            args += list(new)
        outs.append(pl.pallas_call(
            kern, grid=(B_HEADS // 2, n_b, n_qblk), in_specs=in_specs,
            out_specs=pl.BlockSpec((None, 128, LANES), omap),
            out_shape=jax.ShapeDtypeStruct((qbT.shape[0], 512, n_qblk * LANES), MXU_DTYPE),
            compiler_params=_cparams(("arbitrary", "arbitrary", "arbitrary")), name="sb_attn",
        )(*args))
    return outs[0] if len(outs) == 1 else jnp.concatenate(outs, axis=2)


def _top_values(s, n):
    rows = s.shape[0]
    rid = lax.broadcasted_iota(jnp.int32, s.shape, 0).astype(jnp.float32)
    vals = []
    for _ in range(n):
        m = jnp.max(s, axis=0, keepdims=True)
        first = jnp.min(jnp.where(s == m, rid, float(rows)), axis=0, keepdims=True)
        vals.append(m)
        s = jnp.where(rid == first, -jnp.inf, s)
    return vals


def _top_values_untied(s, n):
    n_inf = _col_reduce(jnp.where(s == -jnp.inf, 1.0, 0.0), jnp.sum)
    vals = []
    for _ in range(n):
        m = jnp.max(s, axis=0, keepdims=True)
        vals.append(m)
        s = jnp.where(s == m, -jnp.inf, s)
    extra = _col_reduce(jnp.where(s == -jnp.inf, 1.0, 0.0), jnp.sum) - n_inf - float(n)
    return vals, extra


def _expert_score_stats(s1, s2, top_values):
    r1, r2 = top_values(s1, P_TOPK + 1), top_values(s2, P_TOPK + 1)
    a1, a2 = (r1[0], r2[0]) if isinstance(r1, tuple) else (r1, r2)
    rc = top_values(_candidate_sums(a1, a2), P_TOPK + 1)
    top = rc[0] if isinstance(rc, tuple) else rc
    pad = [jnp.full_like(top[0], -jnp.inf)] * (3 * SUBLANES - (P_TOPK + 1))
    top = jnp.concatenate(top + pad, axis=0)
    if isinstance(rc, tuple):
        return top, jnp.maximum(jnp.maximum(r1[1], r2[1]), rc[1])
    return top


def _candidate_sums(a1, a2):
    tm = a1[0].shape[1]
    ninf = jnp.full((1, tm), -jnp.inf, jnp.float32)
    pad = [ninf] * (3 * SUBLANES - (P_TOPK + 1))
    a1m = jnp.concatenate(a1 + pad, axis=0)
    a2m = jnp.concatenate(a2 + pad, axis=0)
    r8 = lax.broadcasted_iota(jnp.int32, (SUBLANES, tm), 0)
    pieces = [a1[0] + a2m, a1[1] + a2m[0:SUBLANES]]
    for i in range(2, SUBLANES):
        pieces.append(jnp.where(r8 < (P_TOPK + 1) // (i + 1), a1[i] + a2m[0:SUBLANES], -jnp.inf))
    pieces.append(a1m[SUBLANES:] + a2[0])
    return jnp.concatenate(pieces, axis=0)


def _out_proj_kernel(x_ref, oa_ref, ob_ref, sga_ref, sgb_ref, wpa_ref, wpb_ref, wo_ref, n2_ref,
                     wpq_ref, k1_ref, k2_ref,
                     x1_ref, h2_ref, e1_ref, e2_ref, gthr_ref, q_scr, *, x_token_major):
    m = sga_ref[...] * _dot(wpa_ref[...], oa_ref[...]) + sgb_ref[...] * _dot(wpb_ref[...], ob_ref[...])
    x = x_ref[...].T if x_token_major else x_ref[...]
    x1 = x + _dot(wo_ref[...], m.astype(MXU_DTYPE))
    x1_ref[...] = x1
    ms = jnp.mean(x1 * x1, axis=0, keepdims=True)
    h2 = (x1 * lax.rsqrt(ms + EPS) * n2_ref[...]).astype(MXU_DTYPE)
    h2_ref[...] = h2
    q_scr[...] = _dot(wpq_ref[...], h2).astype(MXU_DTYPE)

    def scores(hh):
        r0 = pl.multiple_of(hh * 128, 128)
        s1 = _dot(k1_ref[hh], q_scr[pl.ds(r0, 64), :])
        s2 = _dot(k2_ref[hh], q_scr[pl.ds(pl.multiple_of(r0 + 64, 64), 64), :])
        return (s1, s2) + _expert_score_stats(s1, s2, _top_values_untied)

    def finish(hh, s1, s2, top, extra):
        top = lax.cond(jnp.max(jnp.abs(extra)) > 0.0,
                       lambda: _expert_score_stats(s1, s2, _top_values), lambda: top)
        mx = top[0:1]
        z = jnp.zeros_like(mx)
        for i in range(P_TOPK):
            z = z + jnp.exp(top[i:i + 1] - mx)
        rz = 1.0 / z
        e1_ref[hh] = jnp.exp(s1 - jnp.max(s1, axis=0, keepdims=True)) * rz
        e2_ref[hh] = jnp.exp(s2 - jnp.max(s2, axis=0, keepdims=True))
        gthr_ref[pl.ds(hh, 1), :] = jnp.exp(0.5 * (top[P_TOPK - 1:P_TOPK] + top[P_TOPK:P_TOPK + 1]) - mx) * rz

    def head_pair(i, carry):
        st0, st1 = scores(2 * i), scores(2 * i + 1)
        finish(2 * i, *st0)
        finish(2 * i + 1, *st1)
        return carry

    lax.fori_loop(0, P_HEADS // 2, head_pair, 0)


def _out_proj(x, oaT, obT, sgaT, sgbT, w_paT, w_pbT, w_oT, n2, w_pqT, k1, k2, tm, x_token_major):
    n, d = x.shape if x_token_major else x.shape[::-1]
    nt = n // tm
    tpb = oaT.shape[2] // tm
    f32, bf = jnp.float32, MXU_DTYPE
    tok = lambda i: (0, i)
    tok3 = lambda i: (0, 0, i)
    slab = lambda i: (i // tpb, 0, i % tpb)
    c2 = lambda i: (0, 0)
    c3 = lambda i: (0, 0, 0)
    out_shape = (
        jax.ShapeDtypeStruct((d, n), f32), jax.ShapeDtypeStruct((d, n), bf),
        jax.ShapeDtypeStruct((P_HEADS, N_KEYS, n), f32), jax.ShapeDtypeStruct((P_HEADS, N_KEYS, n), f32),
        jax.ShapeDtypeStruct((P_HEADS, n), f32),
    )
    out_specs = (
        pl.BlockSpec((d, tm), tok), pl.BlockSpec((d, tm), tok),
        pl.BlockSpec((P_HEADS, N_KEYS, tm), tok3), pl.BlockSpec((P_HEADS, N_KEYS, tm), tok3),
        pl.BlockSpec((P_HEADS, tm), tok),
    )
    in_specs = [
        pl.BlockSpec((tm, d), lambda i: (i, 0)) if x_token_major else pl.BlockSpec((d, tm), tok),
        pl.BlockSpec((None, 512, tm), slab), pl.BlockSpec((None, 512, tm), slab),
        pl.BlockSpec((d, tm), tok), pl.BlockSpec((d, tm), tok),
        pl.BlockSpec((d, 512), c2), pl.BlockSpec((d, 512), c2), pl.BlockSpec((d, d), c2),
        pl.BlockSpec((d, 1), c2), pl.BlockSpec((P_HEADS * 128, d), c2),
        pl.BlockSpec((P_HEADS, N_KEYS, P_HALF), c3), pl.BlockSpec((P_HEADS, N_KEYS, P_HALF), c3),
    ]
    return pl.pallas_call(
        functools.partial(_out_proj_kernel, x_token_major=x_token_major),
        grid=(nt,), in_specs=in_specs, out_specs=out_specs, out_shape=out_shape,
        scratch_shapes=[pltpu.VMEM((P_HEADS * 128, tm), bf)],
        compiler_params=_cparams(("parallel",)), name="out_proj",
    )(x, oaT, obT, sgaT, sgbT, w_paT, w_pbT, w_oT, n2, w_pqT, k1, k2)


def _gelu_exact(x):
    return 0.5 * x * (1.0 + lax.erf(x * np.float32(math.sqrt(0.5))))


def _peer_kernel(h2_ref, u_ref, vt_ref, e1_ref, e2_ref, gthr_ref, x1_ref, o_ref, acc_ref, coef_ref, *,
                 out_token_major):
    ec = pl.program_id(1)
    tm = h2_ref.shape[1]

    @pl.when(ec == 0)
    def _():
        acc_ref[...] = jnp.zeros_like(acc_ref)

    a_all = _dot(u_ref[...], h2_ref[...])
    i1_group = pl.ds(pl.multiple_of(ec * SUBLANES, SUBLANES), SUBLANES)
    for c in range(tm // LANES):
        cs = slice(c * LANES, (c + 1) * LANES)
        e1g = [e1_ref[hh, i1_group, cs] for hh in range(P_HEADS)]
        gth = [gthr_ref[hh:hh + 1, cs] for hh in range(P_HEADS)]
        for il in range(SUBLANES):
            gate = jnp.zeros((N_KEYS, LANES), jnp.float32)
            for hh in range(P_HEADS):
                g = e1g[hh][il:il + 1, :] * e2_ref[hh, :, cs]
                gate = gate + jnp.where(g >= gth[hh], g, 0.0)
            a = a_all[il * N_KEYS:(il + 1) * N_KEYS, cs]
            coef_ref[il * N_KEYS:(il + 1) * N_KEYS, cs] = (gate * _gelu_exact(a)).astype(coef_ref.dtype)
    acc_ref[...] += _dot(vt_ref[...], coef_ref[...])

    @pl.when(ec == pl.num_programs(1) - 1)
    def _():
        x2 = x1_ref[...] + acc_ref[...]
        o_ref[...] = x2.T if out_token_major else x2


def _peer(h2T, u, vT, e1T, e2T, gthr, x1T, tm, out_token_major):
    d, n = x1T.shape
    n_exp = u.shape[0]
    tok = lambda i, e: (0, i)
    tok3 = lambda i, e: (0, 0, i)
    in_specs = [
        pl.BlockSpec((d, tm), tok),
        pl.BlockSpec((TE_PEER, d), lambda i, e: (e, 0)),
        pl.BlockSpec((d, TE_PEER), lambda i, e: (0, e)),
        pl.BlockSpec((P_HEADS, N_KEYS, tm), tok3), pl.BlockSpec((P_HEADS, N_KEYS, tm), tok3),
        pl.BlockSpec((P_HEADS, tm), tok),
        pl.BlockSpec((d, tm), tok),
    ]
    return pl.pallas_call(
        functools.partial(_peer_kernel, out_token_major=out_token_major), grid=(n // tm, n_exp // TE_PEER),
        in_specs=in_specs,
        out_specs=pl.BlockSpec((tm, d), lambda i, e: (i, 0)) if out_token_major else pl.BlockSpec((d, tm), tok),
        out_shape=jax.ShapeDtypeStruct((n, d) if out_token_major else (d, n), jnp.float32),
        scratch_shapes=[pltpu.VMEM((d, tm), jnp.float32), pltpu.VMEM((TE_PEER, tm), MXU_DTYPE)],
        compiler_params=_cparams(("parallel", "arbitrary")), name="peer_dense",
    )(h2T, u, vT, e1T, e2T, gthr, x1T)


def _pack_w_in(w_in):
    d = w_in.shape[0]
    cols = [w_in[:, 0:1280], w_in[:, 1280:1352], jnp.zeros((d, 56), w_in.dtype), w_in[:, 1352:]]
    return jnp.concatenate(cols, axis=1).T.astype(MXU_DTYPE)


def _rope_tables(pos):
    half = HD // 2
    inv = ROPE_THETA ** (-jnp.arange(half, dtype=jnp.float32) / half)
    ang = pos.astype(jnp.float32)[None, :] * inv[:, None]
    return jnp.cos(ang), jnp.sin(ang)


def _token_tile(n, pref):
    return pref if n % pref == 0 else LANES


def _layer(x, lw, rope, n_b, t_q, q_pos0, past, topk, x_token_major, out_token_major):
    (n1, w_inT, qn, kn, ikn, w_paT, w_pbT, w_oT, n2, w_pqT, k1, k2, u, vT) = lw
    n = n_b * t_q
    tm = _token_tile(n, TM_PROJ)
    n_bo = n_b if past is None else 1
    (qaT, qiT, wiT, qbT, vaT, vbT, sgaT, sgbT, ka, va, ki, kb, vb) = _in_proj(
        x, n1, w_inT, qn, kn, ikn, rope[0], rope[1], tm, n_bo, x_token_major)
    new = (ka, va, ki, kb, vb)
    if past is None:
        s_valid = t_q
        k_a = ka.reshape(n_b, t_q, 128)
        k_i = ki.reshape(n_b, t_q, 64)
        k_b = kb.reshape(n_b, t_q, 512)
        v_aT, v_bT = vaT, vbT
        new_a = new_b = None
    else:
        k_a, v_aT, k_i, k_b, v_bT = past
        s_valid = k_a.shape[1] + t_q
        new_a, new_b = (ka, va, ki), (kb, vb)
    oaT = _dsa(qaT, qiT, wiT, k_a, v_aT, k_i, new_a, n_b=n_b, t_q=t_q, q_pos0=q_pos0, s_valid=s_valid, topk=topk)
    obT = _sb(qbT, k_b, v_bT, new_b, n_b=n_b, t_q=t_q, q_pos0=q_pos0)
    x1T, h2T, e1T, e2T, gthr = _out_proj(
        x, oaT, obT, sgaT, sgbT, w_paT, w_pbT, w_oT, n2, w_pqT, k1, k2, tm, x_token_major)
    x2 = _peer(h2T, u, vT, e1T, e2T, gthr, x1T, _token_tile(n, TM_PEER), out_token_major)
    return x2, new


def kernel(x_prompt, x_sample, cache_a_k, cache_a_v, cache_idx_k, cache_b_k, cache_b_v, norm1, w_in, q_norm_a, k_norm_a, idx_k_norm, w_pa, w_pb, w_o, norm2, peer_wq, peer_k1, peer_k2, peer_u, peer_v):
    n_bp, t_p, d = x_prompt.shape
    n_bs, t_s, _ = x_sample.shape
    depth = w_in.shape[0]
    past_len = cache_a_k.shape[2]
    assert t_p % LANES == 0 and n_bs * t_s == LANES and past_len % LANES == 0 and t_s % SUBLANES == 0
    assert peer_u.shape[1] % TE_PEER == 0
    topk_p = min(TOPK_MAX, t_p // 4)
    topk_s = min(TOPK_MAX, (past_len + t_s) // 4)
    bf = MXU_DTYPE
    col = lambda g: g.reshape(-1, 1)

    rope_p = _rope_tables(jnp.arange(t_p, dtype=jnp.int32))
    rope_s = _rope_tables(jnp.tile(past_len + jnp.arange(t_s, dtype=jnp.int32), n_bs))

    xp = x_prompt.reshape(n_bp * t_p, d)
    xs = x_sample.reshape(n_bs * t_s, d)
    st_p, st_s = [], []
    for l in range(depth):
        first, last = l == 0, l == depth - 1
        lw = (col(norm1[l]), _pack_w_in(w_in[l]), col(q_norm_a[l]), col(k_norm_a[l]), col(idx_k_norm[l]),
              w_pa[l].T.astype(bf), w_pb[l].T.astype(bf), w_o[l].T.astype(bf), col(norm2[l]),
              peer_wq[l].T.astype(bf), peer_k1[l].astype(bf), peer_k2[l].astype(bf),
              peer_u[l].astype(bf), peer_v[l].T.astype(bf))
        past = (cache_a_k[l].reshape(n_bs, past_len, -1),
                cache_a_v[l].reshape(n_bs, past_len, -1).transpose(0, 2, 1).astype(bf),
                cache_idx_k[l],
                cache_b_k[l].reshape(n_bs, past_len, -1),
                cache_b_v[l].reshape(n_bs, past_len, -1).transpose(0, 2, 1).astype(bf))
        xp, sp = _layer(xp, lw, rope_p, n_bp, t_p, 0, None, topk_p, first, last)
        xs, ss = _layer(xs, lw, rope_s, n_bs, t_s, past_len, past, topk_s, first, last)
        st_p.append(sp)
        st_s.append(ss)

    def stack(sts, j, n_b, t, shape):
        return jnp.stack([s[j] for s in sts]).reshape((depth, n_b, t) + shape)

    outs = [xp.reshape(n_bp, t_p, d), xs.reshape(n_bs, t_s, d)]
    shapes = ((A_KV_HEADS, HD), (A_KV_HEADS, HD), (IDX_DIM,), (B_HEADS, HD), (B_HEADS, HD))
    for sts, n_b, t in ((st_p, n_bp, t_p), (st_s, n_bs, t_s)):
        for j, shp in enumerate(shapes):
            outs.append(stack(sts, j, n_b, t, shp))
    return tuple(outs)
```

```python
import functools
import math

import jax
import jax.numpy as jnp
import numpy as np
from jax import lax
from jax.experimental import pallas as pl
from jax.experimental.pallas import tpu as pltpu

CHUNK = 64
EPS = 1e-6
ROPE_THETA = 10000.0
NEG = -1e30
A_HEADS, A_KV_HEADS, HD = 8, 2, 64
A_GROUP = A_HEADS // A_KV_HEADS
IDX_HEADS, IDX_DIM = 8, 64
TOPK_MAX = 256
B_HEADS = 8
P_HEADS, N_KEYS, P_HALF, P_TOPK = 8, 128, 64, 16

LANES = 128
SUBLANES = 8
VMEM_LIMIT = 56 * 1024 * 1024
TM_PROJ = 256
TM_PEER = 512
TE_PEER = 2 * SUBLANES * N_KEYS
KEY_CHUNK = 512
N_EXTENT_CLASSES = 16

MXU_DTYPE = jnp.bfloat16

SEG_QA, SEG_KA, SEG_VA, SEG_QI, SEG_KIWI = 0, 512, 640, 768, 1280
SEG_QB, SEG_KB, SEG_VB, SEG_GA, SEG_GB, N_IN_PAD = 1408, 1920, 2432, 2944, 3968, 4992


def _cparams(sem):
    return pltpu.CompilerParams(dimension_semantics=sem, vmem_limit_bytes=VMEM_LIMIT)


def _dot(a, b):
    return jnp.dot(a, b, preferred_element_type=jnp.float32)


def _col_reduce(x, op):
    rows = x.shape[0]
    if rows % 64 == 0 and rows > 64:
        x = op(x.reshape(rows // 64, 64, x.shape[1]), axis=0)
    return op(x, axis=0, keepdims=True)


def _rms_rope_head(blk, gain, cos, sin, scale):
    if gain is not None:
        ms = jnp.mean(blk * blk, axis=0, keepdims=True)
        blk = blk * lax.rsqrt(ms + EPS) * gain
    x1, x2 = blk[:32], blk[32:]
    o1 = x1 * cos - x2 * sin
    o2 = x2 * cos + x1 * sin
    if scale != 1.0:
        o1, o2 = o1 * scale, o2 * scale
    return o1, o2


def _in_proj_kernel(x_ref, n1_ref, w_ref, qn_ref, kn_ref, ikn_ref, cos_ref, sin_ref,
                    qa_ref, qi_ref, wi_ref, qb_ref, vat_ref, vbt_ref, sga_ref, sgb_ref,
                    ka_ref, va_ref, ki_ref, kb_ref, vb_ref, *, x_token_major):
    x = x_ref[...].T if x_token_major else x_ref[...]
    ms = jnp.mean(x * x, axis=0, keepdims=True)
    h = (x * lax.rsqrt(ms + EPS) * n1_ref[...]).astype(MXU_DTYPE)
    cos, sin = cos_ref[...], sin_ref[...]
    tm = x.shape[1]

    def seg(start, size):
        return _dot(w_ref[start:start + size, :], h)

    y = seg(SEG_QA, 512)
    for hh in range(A_HEADS):
        o1, o2 = _rms_rope_head(y[hh * 64:(hh + 1) * 64], qn_ref[...], cos, sin, HD ** -0.5)
        qa_ref[hh * 64:hh * 64 + 32, :] = o1.astype(qa_ref.dtype)
        qa_ref[hh * 64 + 32:(hh + 1) * 64, :] = o2.astype(qa_ref.dtype)
    y = seg(SEG_KA, 256)
    parts = []
    for hh in range(A_KV_HEADS):
        o1, o2 = _rms_rope_head(y[hh * 64:(hh + 1) * 64], kn_ref[...], cos, sin, 1.0)
        parts += [o1, o2]
    ka_ref[...] = jnp.concatenate(parts, axis=0).T
    va_t = y[128:256]
    va_ref[...] = va_t.T
    vat_ref[...] = va_t.astype(vat_ref.dtype)
    y = seg(SEG_QI, 512)
    for hh in range(IDX_HEADS):
        o1, o2 = _rms_rope_head(y[hh * 64:(hh + 1) * 64], None, cos, sin, IDX_DIM ** -0.5)
        qi_ref[hh * 64:hh * 64 + 32, :] = o1.astype(qi_ref.dtype)
        qi_ref[hh * 64 + 32:(hh + 1) * 64, :] = o2.astype(qi_ref.dtype)
    y = seg(SEG_KIWI, 128)
    o1, o2 = _rms_rope_head(y[0:64], ikn_ref[...], cos, sin, 1.0)
    ki_t = jnp.concatenate([o1, o2, jnp.zeros((64, tm), jnp.float32)], axis=0)
    ki_ref[...] = ki_t.T[:, :64]
    wi_ref[...] = y[64:72] * (IDX_HEADS ** -0.5)
    qb_ref[...] = (seg(SEG_QB, 512) * (HD ** -0.5)).astype(qb_ref.dtype)
    kb_ref[...] = seg(SEG_KB, 512).T
    y = seg(SEG_VB, 512)
    vb_ref[...] = y.T
    vbt_ref[...] = y.astype(vbt_ref.dtype)
    sga_ref[...] = jax.nn.sigmoid(seg(SEG_GA, 1024))
    sgb_ref[...] = jax.nn.sigmoid(seg(SEG_GB, 1024))


def _in_proj(x, n1, w_inT, qn, kn, ikn, cosT, sinT, tm, n_bo, x_token_major):
    n, d = x.shape if x_token_major else x.shape[::-1]
    nt = n // tm
    t_o = n // n_bo
    tpb = t_o // tm
    pos_blocks = cosT.shape[1] // tm
    f32, bf = jnp.float32, MXU_DTYPE
    tok = lambda i: (0, i)
    row = lambda i: (i, 0)
    const = lambda i: (0, 0)
    slab = lambda i: (i // tpb, 0, i % tpb)
    fm = lambda f, dt: jax.ShapeDtypeStruct((n_bo, f, t_o), dt)
    out_shape = (
        fm(512, bf), fm(512, bf), fm(8, f32), fm(512, bf), fm(128, bf), fm(512, bf),
        jax.ShapeDtypeStruct((1024, n), f32),
        jax.ShapeDtypeStruct((1024, n), f32),
        jax.ShapeDtypeStruct((n, 128), f32),
        jax.ShapeDtypeStruct((n, 128), f32),
        jax.ShapeDtypeStruct((n, 64), f32),
        jax.ShapeDtypeStruct((n, 512), f32),
        jax.ShapeDtypeStruct((n, 512), f32),
    )
    out_specs = (
        pl.BlockSpec((None, 512, tm), slab), pl.BlockSpec((None, 512, tm), slab), pl.BlockSpec((None, 8, tm), slab),
        pl.BlockSpec((None, 512, tm), slab), pl.BlockSpec((None, 128, tm), slab), pl.BlockSpec((None, 512, tm), slab),
        pl.BlockSpec((1024, tm), tok), pl.BlockSpec((1024, tm), tok),
        pl.BlockSpec((tm, 128), row), pl.BlockSpec((tm, 128), row), pl.BlockSpec((tm, 64), row),
        pl.BlockSpec((tm, 512), row), pl.BlockSpec((tm, 512), row),
    )
    in_specs = [
        pl.BlockSpec((tm, d), row) if x_token_major else pl.BlockSpec((d, tm), tok),
        pl.BlockSpec((d, 1), const), pl.BlockSpec((N_IN_PAD, d), const),
        pl.BlockSpec((64, 1), const), pl.BlockSpec((64, 1), const), pl.BlockSpec((64, 1), const),
        pl.BlockSpec((32, tm), lambda i: (0, i % pos_blocks)),
        pl.BlockSpec((32, tm), lambda i: (0, i % pos_blocks)),
    ]
    return pl.pallas_call(
        functools.partial(_in_proj_kernel, x_token_major=x_token_major),
        grid=(nt,), in_specs=in_specs, out_specs=out_specs, out_shape=out_shape,
        compiler_params=_cparams(("parallel",)), name="in_proj",
    )(x, n1, w_inT, qn, kn, ikn, cosT, sinT)


def _extent_classes(t_q, s_total):
    if t_q % LANES:
        return [(0, 1, s_total)]
    nq = t_q // LANES
    cq = max(1, nq // N_EXTENT_CLASSES)
    assert nq % cq == 0
    return [(c * cq, cq, (c + 1) * cq * LANES) for c in range(nq // cq)]


def _query_positions(b, qblk, *, t_q, q_pos0, q_blk0):
    lane = lax.broadcasted_iota(jnp.int32, (1, LANES), 1)
    if t_q % LANES == 0:
        return q_pos0 + (q_blk0 + qblk) * LANES + lane, None
    return q_pos0 + lane % t_q, (lane // t_q) == b


def _store_queries(o_ref, rows, val, valid, b):
    if valid is None:
        o_ref[rows, :] = val
    else:
        @pl.when(b == 0)
        def _():
            o_ref[rows, :] = jnp.zeros(val.shape, o_ref.dtype)
        o_ref[rows, :] = jnp.where(valid, val, o_ref[rows, :])


def _pad_rows(x):
    return jnp.concatenate([x, jnp.zeros((LANES - x.shape[0], x.shape[1]), x.dtype)], axis=0)


def _keys_token_major(cache_ref, new_ref):
    if new_ref is None:
        return cache_ref[...]
    return jnp.concatenate([cache_ref[...], _pad_rows(new_ref[...])], axis=0)


def _values_feature_major(cache_t_ref, new_ref):
    if new_ref is None:
        return cache_t_ref[...]
    return jnp.concatenate([cache_t_ref[...], _pad_rows(new_ref[...]).T.astype(MXU_DTYPE)], axis=1)


def _float_order_key(x):
    bits = lax.bitcast_convert_type(x, jnp.int32)
    key = jnp.where(bits < 0, bits ^ jnp.int32(0x7FFFFFFF), bits)
    return jnp.where(x == 0.0, jnp.int32(0), key)


def _count(mask):
    return _col_reduce(jnp.where(mask, 1.0, 0.0), jnp.sum)


def _dsa_kernel(qa_ref, qi_ref, wi_ref, k_ref, vt_ref, ki_ref, *rest, t_q, q_pos0, q_blk0, s_valid, topk):
    b, qblk = pl.program_id(0), pl.program_id(1)
    (kn_ref, vn_ref, kin_ref), o_ref = (rest[:3] if len(rest) == 4 else (None, None, None)), rest[-1]
    k_all = _keys_token_major(k_ref, kn_ref)
    ki_all = _keys_token_major(ki_ref, kin_ref)
    vt_all = _values_feature_major(vt_ref, vn_ref)
    s_pad = k_all.shape[0]
    qpos, valid = _query_positions(b, qblk, t_q=t_q, q_pos0=q_pos0, q_blk0=q_blk0)
    kpos = lax.broadcasted_iota(jnp.int32, (s_pad, LANES), 0)
    adm = ((kpos // CHUNK) <= (qpos // CHUNK)) & (kpos < s_valid)

    qcat = jnp.concatenate([qi_ref[hh * 64:(hh + 1) * 64, :] for hh in range(IDX_HEADS)], axis=1)
    parts = []
    for r0 in range(0, s_pad, KEY_CHUNK):
        r1 = min(s_pad, r0 + KEY_CHUNK)
        z = _dot(ki_all[r0:r1, :].astype(MXU_DTYPE), qcat)
        acc = jnp.maximum(z[:, :LANES], 0.0) * wi_ref[0:1, :]
        for hh in range(1, IDX_HEADS):
            acc = acc + jnp.maximum(z[:, hh * LANES:(hh + 1) * LANES], 0.0) * wi_ref[hh:hh + 1, :]
        parts.append(acc)
    isc = jnp.concatenate(parts, axis=0) if len(parts) > 1 else parts[0]
    key = _float_order_key(jnp.where(adm, isc, NEG))

    kf = float(topk)
    int_min = jnp.int32(-2 ** 31)
    base = jnp.where(_count(key >= 0) >= kf, jnp.int32(0), int_min)

    def bit_step(i, base):
        cand = base + jnp.left_shift(jnp.int32(1), jnp.int32(30) - i)
        return jnp.where(_count(key >= cand) >= kf, cand, base)

    thr = lax.fori_loop(0, 31, bit_step, base)
    gt = key > thr
    tied = key == thr
    need = kf - _count(gt)
    row = lax.broadcasted_iota(jnp.int32, (LANES, LANES), 0)
    col = lax.broadcasted_iota(jnp.int32, (LANES, LANES), 1)
    tril = jnp.where(col <= row, 1.0, 0.0).astype(MXU_DTYPE)
    tied01 = jnp.where(tied, 1.0, 0.0).astype(MXU_DTYPE)
    carry = jnp.zeros((1, LANES), jnp.float32)
    ranks = []
    for c in range(s_pad // LANES):
        rank = _dot(tril, tied01[c * LANES:(c + 1) * LANES]) + carry
        ranks.append(rank)
        carry = rank[LANES - 1:LANES, :]
    rank = jnp.concatenate(ranks, axis=0) if len(ranks) > 1 else ranks[0]
    sel = gt | (tied & (rank <= need))
    bias = jnp.where(sel & adm, 0.0, NEG)

    kb16 = k_all.astype(MXU_DTYPE)
    zeros64 = jnp.zeros((64, LANES), MXU_DTYPE)
    for n in range(A_KV_HEADS):
        qs = []
        for g in range(A_GROUP):
            hh = n * A_GROUP + g
            qh = qa_ref[hh * 64:(hh + 1) * 64, :]
            qs.append(jnp.concatenate([qh, zeros64] if n == 0 else [zeros64, qh], axis=0))
        s = _dot(kb16, jnp.concatenate(qs, axis=1))
        ps, ls = [], []
        for g in range(A_GROUP):
            sg = s[:, g * LANES:(g + 1) * LANES] + bias
            p = jnp.exp(sg - _col_reduce(sg, jnp.max))
            ls.append(_col_reduce(p, jnp.sum))
            ps.append(p.astype(MXU_DTYPE))
        o = _dot(vt_all[n * 64:(n + 1) * 64, :], jnp.concatenate(ps, axis=1))
        for g in range(A_GROUP):
            hh = n * A_GROUP + g
            og = (o[:, g * LANES:(g + 1) * LANES] / ls[g]).astype(o_ref.dtype)
            _store_queries(o_ref, slice(hh * 64, (hh + 1) * 64), og, valid, b)


def _dsa(qaT, qiT, wiT, k, vT, ki, new, *, n_b, t_q, q_pos0, s_valid, topk):
    shared = t_q % LANES != 0
    s_cache = k.shape[1]
    outs = []
    for q_blk0, n_qblk, s_used in _extent_classes(t_q, s_cache):
        qmap = lambda b, q, q0=q_blk0: (0 if shared else b, 0, q0 + q)
        omap = lambda b, q: (0 if shared else b, 0, q)
        kern = functools.partial(_dsa_kernel, t_q=t_q, q_pos0=q_pos0, q_blk0=q_blk0, s_valid=s_valid, topk=topk)
        in_specs = [
            pl.BlockSpec((None, 512, LANES), qmap), pl.BlockSpec((None, 512, LANES), qmap),
            pl.BlockSpec((None, 8, LANES), qmap),
            pl.BlockSpec((None, s_used, 128), lambda b, q: (b, 0, 0)),
            pl.BlockSpec((None, 128, s_used), lambda b, q: (b, 0, 0)),
            pl.BlockSpec((None, s_used, 64), lambda b, q: (b, 0, 0)),
        ]
        args = [qaT, qiT, wiT, k, vT, ki]
        if new is not None:
            in_specs += [pl.BlockSpec((t_q, f), lambda b, q: (b, 0)) for f in (128, 128, 64)]
            args += list(new)
        outs.append(pl.pallas_call(
            kern, grid=(n_b, n_qblk), in_specs=in_specs,
            out_specs=pl.BlockSpec((None, 512, LANES), omap),
            out_shape=jax.ShapeDtypeStruct((qaT.shape[0], 512, n_qblk * LANES), MXU_DTYPE),
            compiler_params=_cparams(("arbitrary", "arbitrary")), name="dsa_attn",
        )(*args))
    return outs[0] if len(outs) == 1 else jnp.concatenate(outs, axis=2)


def _split3(x):
    top_bits = jnp.int32(-65536)
    trunc = lambda v: lax.bitcast_convert_type(lax.bitcast_convert_type(v, jnp.int32) & top_bits, jnp.float32)
    hi = trunc(x)
    r1 = x - hi
    mid = trunc(r1)
    return hi.astype(MXU_DTYPE), mid.astype(MXU_DTYPE), (r1 - mid).astype(MXU_DTYPE)


def _sb_kernel(q_ref, k_ref, vt_ref, *rest, t_q, q_pos0, q_blk0):
    b, qblk = pl.program_id(1), pl.program_id(2)
    (kn_ref, vn_ref), o_ref = (rest[:2] if len(rest) == 3 else (None, None)), rest[-1]
    k16 = _keys_token_major(k_ref, kn_ref).astype(MXU_DTYPE)
    vt_all = _values_feature_major(vt_ref, vn_ref)
    n_blk = k16.shape[0] // LANES
    qpos, valid = _query_positions(b, qblk, t_q=t_q, q_pos0=q_pos0, q_blk0=q_blk0)
    first_qpos = q_pos0 + (q_blk0 * LANES if valid is None else 0)
    row = lax.broadcasted_iota(jnp.int32, (LANES, LANES), 0)
    col = lax.broadcasted_iota(jnp.int32, (LANES, LANES), 1)
    tri = jnp.where(col > row, 1.0, 0.0).astype(MXU_DTYPE)
    zeros64 = jnp.zeros((64, LANES), MXU_DTYPE)
    for hh in range(2):
        qh = q_ref[hh * 64:(hh + 1) * 64, :]
        z_all = _dot(k16, jnp.concatenate([qh, zeros64] if hh == 0 else [zeros64, qh], axis=0))
        blocks, tots = [], []
        for c in range(n_blk):
            z = z_all[c * LANES:(c + 1) * LANES]
            before = None if (c + 1) * LANES <= first_qpos else (c * LANES + row) < qpos
            ls = jnp.minimum(z, 0.0) - jnp.log(1.0 + jnp.exp(-jnp.abs(z)))
            lr = ls - z
            if before is not None:
                lr = jnp.where(before, lr, 0.0)
            aft = _dot(tri, jnp.concatenate(_split3(lr), axis=1))
            aft = aft[:, :LANES] + aft[:, LANES:2 * LANES] + aft[:, 2 * LANES:]
            blocks.append((ls + aft, before))
            tots.append(jnp.sum(lr, axis=0, keepdims=True))
        rest_lr = jnp.zeros((1, LANES), jnp.float32)
        atts = [None] * n_blk
        for c in reversed(range(n_blk)):
            arg, before = blocks[c]
            att = jnp.exp(arg + rest_lr)
            if before is not None:
                att = jnp.where(before, att, 0.0)
            atts[c] = att.astype(MXU_DTYPE)
            rest_lr = rest_lr + tots[c]
        att = jnp.concatenate(atts, axis=0) if n_blk > 1 else atts[0]
        og = _dot(vt_all[hh * 64:(hh + 1) * 64, :], att).astype(o_ref.dtype)
        _store_queries(o_ref, slice(hh * 64, (hh + 1) * 64), og, valid, b)


def _sb(qbT, kb, vbT, new, *, n_b, t_q, q_pos0):
    shared = t_q % LANES != 0
    outs = []
    for q_blk0, n_qblk, s_used in _extent_classes(t_q, kb.shape[1]):
        qmap = lambda hp, b, q, q0=q_blk0: (0 if shared else b, hp, q0 + q)
        omap = lambda hp, b, q: (0 if shared else b, hp, q)
        kern = functools.partial(_sb_kernel, t_q=t_q, q_pos0=q_pos0, q_blk0=q_blk0)
        in_specs = [
            pl.BlockSpec((None, 128, LANES), qmap),
            pl.BlockSpec((None, s_used, 128), lambda hp, b, q: (b, 0, hp)),
            pl.BlockSpec((None, 128, s_used), lambda hp, b, q: (b, hp, 0)),
        ]
        args = [qbT, kb, vbT]
        if new is not None:
            in_specs += [pl.BlockSpec((t_q, 128), lambda hp, b, q: (b, hp))] * 2
            args += list(new)
        outs.append(pl.pallas_call(
            kern, grid=(B_HEADS // 2, n_b, n_qblk), in_specs=in_specs,
            out_specs=pl.BlockSpec((None, 128, LANES), omap),
            out_shape=jax.ShapeDtypeStruct((qbT.shape[0], 512, n_qblk * LANES), MXU_DTYPE),
            compiler_params=_cparams(("arbitrary", "arbitrary", "arbitrary")), name="sb_attn",
        )(*args))
    return outs[0] if len(outs) == 1 else jnp.concatenate(outs, axis=2)


def _top_values(s, n):
    rows = s.shape[0]
    rid = lax.broadcasted_iota(jnp.int32, s.shape, 0).astype(jnp.float32)
    vals = []
    for _ in range(n):
        m = jnp.max(s, axis=0, keepdims=True)
        first = jnp.min(jnp.where(s == m, rid, float(rows)), axis=0, keepdims=True)
        vals.append(m)
        s = jnp.where(rid == first, -jnp.inf, s)
    return vals


def _top_values_untied(s, n):
    n_inf = _col_reduce(jnp.where(s == -jnp.inf, 1.0, 0.0), jnp.sum)
    vals = []
    for _ in range(n):
        m = jnp.max(s, axis=0, keepdims=True)
        vals.append(m)
        s = jnp.where(s == m, -jnp.inf, s)
    extra = _col_reduce(jnp.where(s == -jnp.inf, 1.0, 0.0), jnp.sum) - n_inf - float(n)
    return vals, extra


def _expert_score_stats(s1, s2, top_values):
    r1, r2 = top_values(s1, P_TOPK + 1), top_values(s2, P_TOPK + 1)
    a1, a2 = (r1[0], r2[0]) if isinstance(r1, tuple) else (r1, r2)
    rc = top_values(_candidate_sums(a1, a2), P_TOPK + 1)
    top = rc[0] if isinstance(rc, tuple) else rc
    pad = [jnp.full_like(top[0], -jnp.inf)] * (3 * SUBLANES - (P_TOPK + 1))
    top = jnp.concatenate(top + pad, axis=0)
    if isinstance(rc, tuple):
        return top, jnp.maximum(jnp.maximum(r1[1], r2[1]), rc[1])
    return top


def _candidate_sums(a1, a2):
    tm = a1[0].shape[1]
    ninf = jnp.full((1, tm), -jnp.inf, jnp.float32)
    pad = [ninf] * (3 * SUBLANES - (P_TOPK + 1))
    a1m = jnp.concatenate(a1 + pad, axis=0)
    a2m = jnp.concatenate(a2 + pad, axis=0)
    r8 = lax.broadcasted_iota(jnp.int32, (SUBLANES, tm), 0)
    pieces = [a1[0] + a2m, a1[1] + a2m[0:SUBLANES]]
    for i in range(2, SUBLANES):
        pieces.append(jnp.where(r8 < (P_TOPK + 1) // (i + 1), a1[i] + a2m[0:SUBLANES], -jnp.inf))
    pieces.append(a1m[SUBLANES:] + a2[0])
    return jnp.concatenate(pieces, axis=0)


def _out_proj_kernel(x_ref, oa_ref, ob_ref, sga_ref, sgb_ref, wpa_ref, wpb_ref, wo_ref, n2_ref,
                     wpq_ref, k1_ref, k2_ref,
                     x1_ref, h2_ref, e1_ref, e2_ref, gthr_ref, q_scr, *, x_token_major):
    m = sga_ref[...] * _dot(wpa_ref[...], oa_ref[...]) + sgb_ref[...] * _dot(wpb_ref[...], ob_ref[...])
    x = x_ref[...].T if x_token_major else x_ref[...]
    x1 = x + _dot(wo_ref[...], m.astype(MXU_DTYPE))
    x1_ref[...] = x1
    ms = jnp.mean(x1 * x1, axis=0, keepdims=True)
    h2 = (x1 * lax.rsqrt(ms + EPS) * n2_ref[...]).astype(MXU_DTYPE)
    h2_ref[...] = h2
    q_scr[...] = _dot(wpq_ref[...], h2).astype(MXU_DTYPE)

    def scores(hh):
        r0 = pl.multiple_of(hh * 128, 128)
        s1 = _dot(k1_ref[hh], q_scr[pl.ds(r0, 64), :])
        s2 = _dot(k2_ref[hh], q_scr[pl.ds(pl.multiple_of(r0 + 64, 64), 64), :])
        return (s1, s2) + _expert_score_stats(s1, s2, _top_values_untied)

    def finish(hh, s1, s2, top, extra):
        top = lax.cond(jnp.max(jnp.abs(extra)) > 0.0,
                       lambda: _expert_score_stats(s1, s2, _top_values), lambda: top)
        mx = top[0:1]
        z = jnp.zeros_like(mx)
        for i in range(P_TOPK):
            z = z + jnp.exp(top[i:i + 1] - mx)
        rz = 1.0 / z
        e1_ref[hh] = jnp.exp(s1 - jnp.max(s1, axis=0, keepdims=True)) * rz
        e2_ref[hh] = jnp.exp(s2 - jnp.max(s2, axis=0, keepdims=True))
        gthr_ref[pl.ds(hh, 1), :] = jnp.exp(0.5 * (top[P_TOPK - 1:P_TOPK] + top[P_TOPK:P_TOPK + 1]) - mx) * rz

    def head_pair(i, carry):
        st0, st1 = scores(2 * i), scores(2 * i + 1)
        finish(2 * i, *st0)
        finish(2 * i + 1, *st1)
        return carry

    lax.fori_loop(0, P_HEADS // 2, head_pair, 0)


def _out_proj(x, oaT, obT, sgaT, sgbT, w_paT, w_pbT, w_oT, n2, w_pqT, k1, k2, tm, x_token_major):
    n, d = x.shape if x_token_major else x.shape[::-1]
    nt = n // tm
    tpb = oaT.shape[2] // tm
    f32, bf = jnp.float32, MXU_DTYPE
    tok = lambda i: (0, i)
    tok3 = lambda i: (0, 0, i)
    slab = lambda i: (i // tpb, 0, i % tpb)
    c2 = lambda i: (0, 0)
    c3 = lambda i: (0, 0, 0)
    out_shape = (
        jax.ShapeDtypeStruct((d, n), f32), jax.ShapeDtypeStruct((d, n), bf),
        jax.ShapeDtypeStruct((P_HEADS, N_KEYS, n), f32), jax.ShapeDtypeStruct((P_HEADS, N_KEYS, n), f32),
        jax.ShapeDtypeStruct((P_HEADS, n), f32),
    )
    out_specs = (
        pl.BlockSpec((d, tm), tok), pl.BlockSpec((d, tm), tok),
        pl.BlockSpec((P_HEADS, N_KEYS, tm), tok3), pl.BlockSpec((P_HEADS, N_KEYS, tm), tok3),
        pl.BlockSpec((P_HEADS, tm), tok),
    )
    in_specs = [
        pl.BlockSpec((tm, d), lambda i: (i, 0)) if x_token_major else pl.BlockSpec((d, tm), tok),
        pl.BlockSpec((None, 512, tm), slab), pl.BlockSpec((None, 512, tm), slab),
        pl.BlockSpec((d, tm), tok), pl.BlockSpec((d, tm), tok),
        pl.BlockSpec((d, 512), c2), pl.BlockSpec((d, 512), c2), pl.BlockSpec((d, d), c2),
        pl.BlockSpec((d, 1), c2), pl.BlockSpec((P_HEADS * 128, d), c2),
        pl.BlockSpec((P_HEADS, N_KEYS, P_HALF), c3), pl.BlockSpec((P_HEADS, N_KEYS, P_HALF), c3),
    ]
    return pl.pallas_call(
        functools.partial(_out_proj_kernel, x_token_major=x_token_major),
        grid=(nt,), in_specs=in_specs, out_specs=out_specs, out_shape=out_shape,
        scratch_shapes=[pltpu.VMEM((P_HEADS * 128, tm), bf)],
        compiler_params=_cparams(("parallel",)), name="out_proj",
    )(x, oaT, obT, sgaT, sgbT, w_paT, w_pbT, w_oT, n2, w_pqT, k1, k2)


def _gelu_exact(x):
    return 0.5 * x * (1.0 + lax.erf(x * np.float32(math.sqrt(0.5))))


def _peer_kernel(h2_ref, u_ref, vt_ref, e1_ref, e2_ref, gthr_ref, x1_ref, o_ref, acc_ref, coef_ref, *,
                 out_token_major):
    ec = pl.program_id(1)
    tm = h2_ref.shape[1]

    @pl.when(ec == 0)
    def _():
        acc_ref[...] = jnp.zeros_like(acc_ref)

    a_all = _dot(u_ref[...], h2_ref[...])
    n_groups = TE_PEER // (SUBLANES * N_KEYS)
    for grp in range(n_groups):
        i1_group = pl.ds(pl.multiple_of((ec * n_groups + grp) * SUBLANES, SUBLANES), SUBLANES)
        for c in range(tm // LANES):
            cs = slice(c * LANES, (c + 1) * LANES)
            e1g = [e1_ref[hh, i1_group, cs] for hh in range(P_HEADS)]
            gth = [gthr_ref[hh:hh + 1, cs] for hh in range(P_HEADS)]
            for il in range(SUBLANES):
                gate = jnp.zeros((N_KEYS, LANES), jnp.float32)
                for hh in range(P_HEADS):
                    g = e1g[hh][il:il + 1, :] * e2_ref[hh, :, cs]
                    gate = gate + jnp.where(g >= gth[hh], g, 0.0)
                rows = slice((grp * SUBLANES + il) * N_KEYS, (grp * SUBLANES + il + 1) * N_KEYS)
                coef_ref[rows, cs] = (gate * _gelu_exact(a_all[rows, cs])).astype(coef_ref.dtype)
    acc_ref[...] += _dot(vt_ref[...], coef_ref[...])

    @pl.when(ec == pl.num_programs(1) - 1)
    def _():
        x2 = x1_ref[...] + acc_ref[...]
        o_ref[...] = x2.T if out_token_major else x2


def _peer(h2T, u, vT, e1T, e2T, gthr, x1T, tm, out_token_major):
    d, n = x1T.shape
    n_exp = u.shape[0]
    tok = lambda i, e: (0, i)
    tok3 = lambda i, e: (0, 0, i)
    in_specs = [
        pl.BlockSpec((d, tm), tok),
        pl.BlockSpec((TE_PEER, d), lambda i, e: (e, 0)),
        pl.BlockSpec((d, TE_PEER), lambda i, e: (0, e)),
        pl.BlockSpec((P_HEADS, N_KEYS, tm), tok3), pl.BlockSpec((P_HEADS, N_KEYS, tm), tok3),
        pl.BlockSpec((P_HEADS, tm), tok),
        pl.BlockSpec((d, tm), tok),
    ]
    return pl.pallas_call(
        functools.partial(_peer_kernel, out_token_major=out_token_major), grid=(n // tm, n_exp // TE_PEER),
        in_specs=in_specs,
        out_specs=pl.BlockSpec((tm, d), lambda i, e: (i, 0)) if out_token_major else pl.BlockSpec((d, tm), tok),
        out_shape=jax.ShapeDtypeStruct((n, d) if out_token_major else (d, n), jnp.float32),
        scratch_shapes=[pltpu.VMEM((d, tm), jnp.float32), pltpu.VMEM((TE_PEER, tm), MXU_DTYPE)],
        compiler_params=_cparams(("parallel", "arbitrary")), name="peer_dense",
    )(h2T, u, vT, e1T, e2T, gthr, x1T)


def _pack_w_in(w_in):
    d = w_in.shape[0]
    cols = [w_in[:, 0:1280], w_in[:, 1280:1352], jnp.zeros((d, 56), w_in.dtype), w_in[:, 1352:]]
    return jnp.concatenate(cols, axis=1).T.astype(MXU_DTYPE)


def _rope_tables(pos):
    half = HD // 2
    inv = ROPE_THETA ** (-jnp.arange(half, dtype=jnp.float32) / half)
    ang = pos.astype(jnp.float32)[None, :] * inv[:, None]
    return jnp.cos(ang), jnp.sin(ang)


def _token_tile(n, pref):
    return pref if n % pref == 0 else LANES


def _layer(x, lw, rope, n_b, t_q, q_pos0, past, topk, x_token_major, out_token_major):
    (n1, w_inT, qn, kn, ikn, w_paT, w_pbT, w_oT, n2, w_pqT, k1, k2, u, vT) = lw
    n = n_b * t_q
    tm = _token_tile(n, TM_PROJ)
    n_bo = n_b if past is None else 1
    (qaT, qiT, wiT, qbT, vaT, vbT, sgaT, sgbT, ka, va, ki, kb, vb) = _in_proj(
        x, n1, w_inT, qn, kn, ikn, rope[0], rope[1], tm, n_bo, x_token_major)
    new = (ka, va, ki, kb, vb)
    if past is None:
        s_valid = t_q
        k_a = ka.reshape(n_b, t_q, 128)
        k_i = ki.reshape(n_b, t_q, 64)
        k_b = kb.reshape(n_b, t_q, 512)
        v_aT, v_bT = vaT, vbT
        new_a = new_b = None
    else:
        k_a, v_aT, k_i, k_b, v_bT = past
        s_valid = k_a.shape[1] + t_q
        new_a, new_b = (ka, va, ki), (kb, vb)
    oaT = _dsa(qaT, qiT, wiT, k_a, v_aT, k_i, new_a, n_b=n_b, t_q=t_q, q_pos0=q_pos0, s_valid=s_valid, topk=topk)
    obT = _sb(qbT, k_b, v_bT, new_b, n_b=n_b, t_q=t_q, q_pos0=q_pos0)
    x1T, h2T, e1T, e2T, gthr = _out_proj(
        x, oaT, obT, sgaT, sgbT, w_paT, w_pbT, w_oT, n2, w_pqT, k1, k2, tm, x_token_major)
    x2 = _peer(h2T, u, vT, e1T, e2T, gthr, x1T, _token_tile(n, TM_PEER), out_token_major)
    return x2, new


def kernel(x_prompt, x_sample, cache_a_k, cache_a_v, cache_idx_k, cache_b_k, cache_b_v, norm1, w_in, q_norm_a, k_norm_a, idx_k_norm, w_pa, w_pb, w_o, norm2, peer_wq, peer_k1, peer_k2, peer_u, peer_v):
    n_bp, t_p, d = x_prompt.shape
    n_bs, t_s, _ = x_sample.shape
    depth = w_in.shape[0]
    past_len = cache_a_k.shape[2]
    assert t_p % LANES == 0 and n_bs * t_s == LANES and past_len % LANES == 0 and t_s % SUBLANES == 0
    assert peer_u.shape[1] % TE_PEER == 0
    topk_p = min(TOPK_MAX, t_p // 4)
    topk_s = min(TOPK_MAX, (past_len + t_s) // 4)
    bf = MXU_DTYPE
    col = lambda g: g.reshape(-1, 1)

    rope_p = _rope_tables(jnp.arange(t_p, dtype=jnp.int32))
    rope_s = _rope_tables(jnp.tile(past_len + jnp.arange(t_s, dtype=jnp.int32), n_bs))

    xp = x_prompt.reshape(n_bp * t_p, d)
    xs = x_sample.reshape(n_bs * t_s, d)
    st_p, st_s = [], []
    for l in range(depth):
        first, last = l == 0, l == depth - 1
        lw = (col(norm1[l]), _pack_w_in(w_in[l]), col(q_norm_a[l]), col(k_norm_a[l]), col(idx_k_norm[l]),
              w_pa[l].T.astype(bf), w_pb[l].T.astype(bf), w_o[l].T.astype(bf), col(norm2[l]),
              peer_wq[l].T.astype(bf), peer_k1[l].astype(bf), peer_k2[l].astype(bf),
              peer_u[l].astype(bf), peer_v[l].T.astype(bf))
        past = (cache_a_k[l].reshape(n_bs, past_len, -1),
                cache_a_v[l].reshape(n_bs, past_len, -1).transpose(0, 2, 1).astype(bf),
                cache_idx_k[l],
                cache_b_k[l].reshape(n_bs, past_len, -1),
                cache_b_v[l].reshape(n_bs, past_len, -1).transpose(0, 2, 1).astype(bf))
        xp, sp = _layer(xp, lw, rope_p, n_bp, t_p, 0, None, topk_p, first, last)
        xs, ss = _layer(xs, lw, rope_s, n_bs, t_s, past_len, past, topk_s, first, last)
        st_p.append(sp)
        st_s.append(ss)

    def stack(sts, j, n_b, t, shape):
        return jnp.stack([s[j] for s in sts]).reshape((depth, n_b, t) + shape)

    outs = [xp.reshape(n_bp, t_p, d), xs.reshape(n_bs, t_s, d)]
    shapes = ((A_KV_HEADS, HD), (A_KV_HEADS, HD), (IDX_DIM,), (B_HEADS, HD), (B_HEADS, HD))
    for sts, n_b, t in ((st_p, n_bp, t_p), (st_s, n_bs, t_s)):
        for j, shp in enumerate(shapes):
            outs.append(stack(sts, j, n_b, t, shp))
    return tuple(outs)
```

```python
import functools
import math

import jax
import jax.numpy as jnp
import numpy as np
from jax import lax
from jax.experimental import pallas as pl
from jax.experimental.pallas import tpu as pltpu

CHUNK = 64
EPS = 1e-6
ROPE_THETA = 10000.0
NEG = -1e30
A_HEADS, A_KV_HEADS, HD = 8, 2, 64
A_GROUP = A_HEADS // A_KV_HEADS
IDX_HEADS, IDX_DIM = 8, 64
TOPK_MAX = 256
B_HEADS = 8
P_HEADS, N_KEYS, P_HALF, P_TOPK = 8, 128, 64, 16

LANES = 128
SUBLANES = 8
VMEM_LIMIT = 56 * 1024 * 1024
TM_PROJ = 256
TM_PEER = 512
TE_PEER = 2 * SUBLANES * N_KEYS
KEY_CHUNK = 512
HEADS_PER_TRIP = 4
N_EXTENT_CLASSES = 16

MXU_DTYPE = jnp.bfloat16

SEG_QA, SEG_KA, SEG_VA, SEG_QI, SEG_KIWI = 0, 512, 640, 768, 1280
SEG_QB, SEG_KB, SEG_VB, SEG_GA, SEG_GB, N_IN_PAD = 1408, 1920, 2432, 2944, 3968, 4992


def _cparams(sem):
    return pltpu.CompilerParams(dimension_semantics=sem, vmem_limit_bytes=VMEM_LIMIT)


def _dot(a, b):
    return jnp.dot(a, b, preferred_element_type=jnp.float32)


def _col_reduce(x, op):
    rows = x.shape[0]
    if rows % 64 == 0 and rows > 64:
        x = op(x.reshape(rows // 64, 64, x.shape[1]), axis=0)
    return op(x, axis=0, keepdims=True)


def _rms_rope_head(blk, gain, cos, sin, scale):
    if gain is not None:
        ms = jnp.mean(blk * blk, axis=0, keepdims=True)
        blk = blk * lax.rsqrt(ms + EPS) * gain
    x1, x2 = blk[:32], blk[32:]
    o1 = x1 * cos - x2 * sin
    o2 = x2 * cos + x1 * sin
    if scale != 1.0:
        o1, o2 = o1 * scale, o2 * scale
    return o1, o2


def _in_proj_kernel(x_ref, n1_ref, w_ref, qn_ref, kn_ref, ikn_ref, cos_ref, sin_ref,
                    qa_ref, qi_ref, wi_ref, qb_ref, vat_ref, vbt_ref, sga_ref, sgb_ref,
                    ka_ref, va_ref, ki_ref, kb_ref, vb_ref, *, x_token_major):
    x = x_ref[...].T if x_token_major else x_ref[...]
    ms = jnp.mean(x * x, axis=0, keepdims=True)
    h = (x * lax.rsqrt(ms + EPS) * n1_ref[...]).astype(MXU_DTYPE)
    cos, sin = cos_ref[...], sin_ref[...]
    tm = x.shape[1]

    def seg(start, size):
        return _dot(w_ref[start:start + size, :], h)

    y = seg(SEG_QA, 512)
    for hh in range(A_HEADS):
        o1, o2 = _rms_rope_head(y[hh * 64:(hh + 1) * 64], qn_ref[...], cos, sin, HD ** -0.5)
        qa_ref[hh * 64:hh * 64 + 32, :] = o1.astype(qa_ref.dtype)
        qa_ref[hh * 64 + 32:(hh + 1) * 64, :] = o2.astype(qa_ref.dtype)
    y = seg(SEG_KA, 256)
    parts = []
    for hh in range(A_KV_HEADS):
        o1, o2 = _rms_rope_head(y[hh * 64:(hh + 1) * 64], kn_ref[...], cos, sin, 1.0)
        parts += [o1, o2]
    ka_ref[...] = jnp.concatenate(parts, axis=0).T
    va_t = y[128:256]
    va_ref[...] = va_t.T
    vat_ref[...] = va_t.astype(vat_ref.dtype)
    y = seg(SEG_QI, 512)
    for hh in range(IDX_HEADS):
        o1, o2 = _rms_rope_head(y[hh * 64:(hh + 1) * 64], None, cos, sin, IDX_DIM ** -0.5)
        qi_ref[hh * 64:hh * 64 + 32, :] = o1.astype(qi_ref.dtype)
        qi_ref[hh * 64 + 32:(hh + 1) * 64, :] = o2.astype(qi_ref.dtype)
    y = seg(SEG_KIWI, 128)
    o1, o2 = _rms_rope_head(y[0:64], ikn_ref[...], cos, sin, 1.0)
    ki_t = jnp.concatenate([o1, o2, jnp.zeros((64, tm), jnp.float32)], axis=0)
    ki_ref[...] = ki_t.T[:, :64]
    wi_ref[...] = y[64:72] * (IDX_HEADS ** -0.5)
    qb_ref[...] = (seg(SEG_QB, 512) * (HD ** -0.5)).astype(qb_ref.dtype)
    kb_ref[...] = seg(SEG_KB, 512).T
    y = seg(SEG_VB, 512)
    vb_ref[...] = y.T
    vbt_ref[...] = y.astype(vbt_ref.dtype)
    sga_ref[...] = jax.nn.sigmoid(seg(SEG_GA, 1024))
    sgb_ref[...] = jax.nn.sigmoid(seg(SEG_GB, 1024))


def _in_proj(x, n1, w_inT, qn, kn, ikn, cosT, sinT, tm, n_bo, x_token_major):
    n, d = x.shape if x_token_major else x.shape[::-1]
    nt = n // tm
    t_o = n // n_bo
    tpb = t_o // tm
    pos_blocks = cosT.shape[1] // tm
    f32, bf = jnp.float32, MXU_DTYPE
    tok = lambda i: (0, i)
    row = lambda i: (i, 0)
    const = lambda i: (0, 0)
    slab = lambda i: (i // tpb, 0, i % tpb)
    fm = lambda f, dt: jax.ShapeDtypeStruct((n_bo, f, t_o), dt)
    out_shape = (
        fm(512, bf), fm(512, bf), fm(8, f32), fm(512, bf), fm(128, bf), fm(512, bf),
        jax.ShapeDtypeStruct((1024, n), f32),
        jax.ShapeDtypeStruct((1024, n), f32),
        jax.ShapeDtypeStruct((n, 128), f32),
        jax.ShapeDtypeStruct((n, 128), f32),
        jax.ShapeDtypeStruct((n, 64), f32),
        jax.ShapeDtypeStruct((n, 512), f32),
        jax.ShapeDtypeStruct((n, 512), f32),
    )
    out_specs = (
        pl.BlockSpec((None, 512, tm), slab), pl.BlockSpec((None, 512, tm), slab), pl.BlockSpec((None, 8, tm), slab),
        pl.BlockSpec((None, 512, tm), slab), pl.BlockSpec((None, 128, tm), slab), pl.BlockSpec((None, 512, tm), slab),
        pl.BlockSpec((1024, tm), tok), pl.BlockSpec((1024, tm), tok),
        pl.BlockSpec((tm, 128), row), pl.BlockSpec((tm, 128), row), pl.BlockSpec((tm, 64), row),
        pl.BlockSpec((tm, 512), row), pl.BlockSpec((tm, 512), row),
    )
    in_specs = [
        pl.BlockSpec((tm, d), row) if x_token_major else pl.BlockSpec((d, tm), tok),
        pl.BlockSpec((d, 1), const), pl.BlockSpec((N_IN_PAD, d), const),
        pl.BlockSpec((64, 1), const), pl.BlockSpec((64, 1), const), pl.BlockSpec((64, 1), const),
        pl.BlockSpec((32, tm), lambda i: (0, i % pos_blocks)),
        pl.BlockSpec((32, tm), lambda i: (0, i % pos_blocks)),
    ]
    return pl.pallas_call(
        functools.partial(_in_proj_kernel, x_token_major=x_token_major),
        grid=(nt,), in_specs=in_specs, out_specs=out_specs, out_shape=out_shape,
        compiler_params=_cparams(("parallel",)), name="in_proj",
    )(x, n1, w_inT, qn, kn, ikn, cosT, sinT)


def _extent_classes(t_q, s_total):
    if t_q % LANES:
        return [(0, 1, s_total)]
    nq = t_q // LANES
    cq = max(1, nq // N_EXTENT_CLASSES)
    assert nq % cq == 0
    return [(c * cq, cq, (c + 1) * cq * LANES) for c in range(nq // cq)]


def _query_positions(b, qblk, *, t_q, q_pos0, q_blk0):
    lane = lax.broadcasted_iota(jnp.int32, (1, LANES), 1)
    if t_q % LANES == 0:
        return q_pos0 + (q_blk0 + qblk) * LANES + lane, None
    return q_pos0 + lane % t_q, (lane // t_q) == b


def _store_queries(o_ref, rows, val, valid, b):
    if valid is None:
        o_ref[rows, :] = val
    else:
        @pl.when(b == 0)
        def _():
            o_ref[rows, :] = jnp.zeros(val.shape, o_ref.dtype)
        o_ref[rows, :] = jnp.where(valid, val, o_ref[rows, :])


def _pad_rows(x):
    return jnp.concatenate([x, jnp.zeros((LANES - x.shape[0], x.shape[1]), x.dtype)], axis=0)


def _keys_token_major(cache_ref, new_ref):
    if new_ref is None:
        return cache_ref[...]
    return jnp.concatenate([cache_ref[...], _pad_rows(new_ref[...])], axis=0)


def _values_feature_major(cache_t_ref, new_ref):
    if new_ref is None:
        return cache_t_ref[...]
    return jnp.concatenate([cache_t_ref[...], _pad_rows(new_ref[...]).T.astype(MXU_DTYPE)], axis=1)


def _float_order_key(x):
    bits = lax.bitcast_convert_type(x, jnp.int32)
    key = jnp.where(bits < 0, bits ^ jnp.int32(0x7FFFFFFF), bits)
    return jnp.where(x == 0.0, jnp.int32(0), key)


def _count(mask):
    return _col_reduce(jnp.where(mask, 1.0, 0.0), jnp.sum)


def _dsa_kernel(qa_ref, qi_ref, wi_ref, k_ref, vt_ref, ki_ref, *rest, t_q, q_pos0, q_blk0, s_valid, topk):
    b, qblk = pl.program_id(0), pl.program_id(1)
    (kn_ref, vn_ref, kin_ref), o_ref = (rest[:3] if len(rest) == 4 else (None, None, None)), rest[-1]
    k_all = _keys_token_major(k_ref, kn_ref)
    ki_all = _keys_token_major(ki_ref, kin_ref)
    vt_all = _values_feature_major(vt_ref, vn_ref)
    s_pad = k_all.shape[0]
    qpos, valid = _query_positions(b, qblk, t_q=t_q, q_pos0=q_pos0, q_blk0=q_blk0)
    kpos = lax.broadcasted_iota(jnp.int32, (s_pad, LANES), 0)
    adm = ((kpos // CHUNK) <= (qpos // CHUNK)) & (kpos < s_valid)

    qcat = jnp.concatenate([qi_ref[hh * 64:(hh + 1) * 64, :] for hh in range(IDX_HEADS)], axis=1)
    parts = []
    for r0 in range(0, s_pad, KEY_CHUNK):
        r1 = min(s_pad, r0 + KEY_CHUNK)
        z = _dot(ki_all[r0:r1, :].astype(MXU_DTYPE), qcat)
        acc = jnp.maximum(z[:, :LANES], 0.0) * wi_ref[0:1, :]
        for hh in range(1, IDX_HEADS):
            acc = acc + jnp.maximum(z[:, hh * LANES:(hh + 1) * LANES], 0.0) * wi_ref[hh:hh + 1, :]
        parts.append(acc)
    isc = jnp.concatenate(parts, axis=0) if len(parts) > 1 else parts[0]
    key = _float_order_key(jnp.where(adm, isc, NEG))

    kf = float(topk)
    int_min = jnp.int32(-2 ** 31)
    base = jnp.where(_count(key >= 0) >= kf, jnp.int32(0), int_min)

    def bit_step(i, base):
        cand = base + jnp.left_shift(jnp.int32(1), jnp.int32(30) - i)
        return jnp.where(_count(key >= cand) >= kf, cand, base)

    thr = lax.fori_loop(0, 31, bit_step, base)
    gt = key > thr
    tied = key == thr
    need = kf - _count(gt)
    row = lax.broadcasted_iota(jnp.int32, (LANES, LANES), 0)
    col = lax.broadcasted_iota(jnp.int32, (LANES, LANES), 1)
    tril = jnp.where(col <= row, 1.0, 0.0).astype(MXU_DTYPE)
    tied01 = jnp.where(tied, 1.0, 0.0).astype(MXU_DTYPE)
    carry = jnp.zeros((1, LANES), jnp.float32)
    ranks = []
    for c in range(s_pad // LANES):
        rank = _dot(tril, tied01[c * LANES:(c + 1) * LANES]) + carry
        ranks.append(rank)
        carry = rank[LANES - 1:LANES, :]
    rank = jnp.concatenate(ranks, axis=0) if len(ranks) > 1 else ranks[0]
    sel = gt | (tied & (rank <= need))
    bias = jnp.where(sel & adm, 0.0, NEG)

    kb16 = k_all.astype(MXU_DTYPE)
    zeros64 = jnp.zeros((64, LANES), MXU_DTYPE)
    for n in range(A_KV_HEADS):
        qs = []
        for g in range(A_GROUP):
            hh = n * A_GROUP + g
            qh = qa_ref[hh * 64:(hh + 1) * 64, :]
            qs.append(jnp.concatenate([qh, zeros64] if n == 0 else [zeros64, qh], axis=0))
        s = _dot(kb16, jnp.concatenate(qs, axis=1))
        ps, ls = [], []
        for g in range(A_GROUP):
            sg = s[:, g * LANES:(g + 1) * LANES] + bias
            p = jnp.exp(sg - _col_reduce(sg, jnp.max))
            ls.append(_col_reduce(p, jnp.sum))
            ps.append(p.astype(MXU_DTYPE))
        o = _dot(vt_all[n * 64:(n + 1) * 64, :], jnp.concatenate(ps, axis=1))
        for g in range(A_GROUP):
            hh = n * A_GROUP + g
            og = (o[:, g * LANES:(g + 1) * LANES] / ls[g]).astype(o_ref.dtype)
            _store_queries(o_ref, slice(hh * 64, (hh + 1) * 64), og, valid, b)


def _dsa(qaT, qiT, wiT, k, vT, ki, new, *, n_b, t_q, q_pos0, s_valid, topk):
    shared = t_q % LANES != 0
    s_cache = k.shape[1]
    outs = []
    for q_blk0, n_qblk, s_used in _extent_classes(t_q, s_cache):
        qmap = lambda b, q, q0=q_blk0: (0 if shared else b, 0, q0 + q)
        omap = lambda b, q: (0 if shared else b, 0, q)
        kern = functools.partial(_dsa_kernel, t_q=t_q, q_pos0=q_pos0, q_blk0=q_blk0, s_valid=s_valid, topk=topk)
        in_specs = [
            pl.BlockSpec((None, 512, LANES), qmap), pl.BlockSpec((None, 512, LANES), qmap),
            pl.BlockSpec((None, 8, LANES), qmap),
            pl.BlockSpec((None, s_used, 128), lambda b, q: (b, 0, 0)),
            pl.BlockSpec((None, 128, s_used), lambda b, q: (b, 0, 0)),
            pl.BlockSpec((None, s_used, 64), lambda b, q: (b, 0, 0)),
        ]
        args = [qaT, qiT, wiT, k, vT, ki]
        if new is not None:
            in_specs += [pl.BlockSpec((t_q, f), lambda b, q: (b, 0)) for f in (128, 128, 64)]
            args += list(new)
        outs.append(pl.pallas_call(
            kern, grid=(n_b, n_qblk), in_specs=in_specs,
            out_specs=pl.BlockSpec((None, 512, LANES), omap),
            out_shape=jax.ShapeDtypeStruct((qaT.shape[0], 512, n_qblk * LANES), MXU_DTYPE),
            compiler_params=_cparams(("arbitrary", "arbitrary")), name="dsa_attn",
        )(*args))
    return outs[0] if len(outs) == 1 else jnp.concatenate(outs, axis=2)


def _split3(x):
    top_bits = jnp.int32(-65536)
    trunc = lambda v: lax.bitcast_convert_type(lax.bitcast_convert_type(v, jnp.int32) & top_bits, jnp.float32)
    hi = trunc(x)
    r1 = x - hi
    mid = trunc(r1)
    return hi.astype(MXU_DTYPE), mid.astype(MXU_DTYPE), (r1 - mid).astype(MXU_DTYPE)


def _sb_weights(z_all, qpos, first_qpos):
    n_blk = z_all.shape[0] // LANES
    row = lax.broadcasted_iota(jnp.int32, (LANES, LANES), 0)
    col = lax.broadcasted_iota(jnp.int32, (LANES, LANES), 1)
    tri = jnp.where(col > row, 1.0, 0.0).astype(MXU_DTYPE)
    blocks, tots = [], []
    for c in range(n_blk):
        z = z_all[c * LANES:(c + 1) * LANES]
        before = None if (c + 1) * LANES <= first_qpos else (c * LANES + row) < qpos
        ls = jnp.minimum(z, 0.0) - jnp.log(1.0 + jnp.exp(-jnp.abs(z)))
        lr = ls - z
        if before is not None:
            lr = jnp.where(before, lr, 0.0)
        aft = _dot(tri, jnp.concatenate(_split3(lr), axis=1))
        aft = aft[:, :LANES] + aft[:, LANES:2 * LANES] + aft[:, 2 * LANES:]
        blocks.append((ls + aft, before))
        tots.append(jnp.sum(lr, axis=0, keepdims=True))
    rest_lr = jnp.zeros((1, LANES), jnp.float32)
    atts = [None] * n_blk
    for c in reversed(range(n_blk)):
        arg, before = blocks[c]
        att = jnp.exp(arg + rest_lr)
        if before is not None:
            att = jnp.where(before, att, 0.0)
        atts[c] = att.astype(MXU_DTYPE)
        rest_lr = rest_lr + tots[c]
    return jnp.concatenate(atts, axis=0) if n_blk > 1 else atts[0]


def _sb_kernel(q_ref, k_ref, vt_ref, o_ref, *, q_blk0):
    qblk = pl.program_id(2)
    k16 = k_ref[...].astype(MXU_DTYPE)
    qpos, _ = _query_positions(0, qblk, t_q=LANES, q_pos0=0, q_blk0=q_blk0)
    zeros64 = jnp.zeros((64, LANES), MXU_DTYPE)
    for hh in range(2):
        qh = q_ref[hh * 64:(hh + 1) * 64, :]
        z_all = _dot(k16, jnp.concatenate([qh, zeros64] if hh == 0 else [zeros64, qh], axis=0))
        att = _sb_weights(z_all, qpos, q_blk0 * LANES)
        o_ref[hh * 64:(hh + 1) * 64, :] = _dot(vt_ref[hh * 64:(hh + 1) * 64, :], att).astype(o_ref.dtype)


def _sb_step_kernel(q_ref, k_ref, vt_ref, kn_ref, vn_ref, o_ref, *, t_q, q_pos0):
    k16 = _keys_token_major(k_ref, kn_ref).astype(MXU_DTYPE)
    vt_all = _values_feature_major(vt_ref, vn_ref)
    lane = lax.broadcasted_iota(jnp.int32, (1, LANES), 1)
    att = _sb_weights(_dot(k16, q_ref[...]), q_pos0 + lane % t_q, q_pos0)
    o_ref[...] = _dot(vt_all, att).astype(o_ref.dtype)


def _sb(qbT, kb, vbT, *, n_b, t_q):
    outs = []
    for q_blk0, n_qblk, s_used in _extent_classes(t_q, kb.shape[1]):
        outs.append(pl.pallas_call(
            functools.partial(_sb_kernel, q_blk0=q_blk0), grid=(B_HEADS // 2, n_b, n_qblk),
            in_specs=[
                pl.BlockSpec((None, 128, LANES), lambda hp, b, q, q0=q_blk0: (b, hp, q0 + q)),
                pl.BlockSpec((None, s_used, 128), lambda hp, b, q: (b, 0, hp)),
                pl.BlockSpec((None, 128, s_used), lambda hp, b, q: (b, hp, 0)),
            ],
            out_specs=pl.BlockSpec((None, 128, LANES), lambda hp, b, q: (b, hp, q)),
            out_shape=jax.ShapeDtypeStruct((n_b, 512, n_qblk * LANES), MXU_DTYPE),
            compiler_params=_cparams(("parallel", "parallel", "parallel")), name="sb_attn",
        )(qbT, kb, vbT))
    return outs[0] if len(outs) == 1 else jnp.concatenate(outs, axis=2)


def _sb_step(qbT, kb, vbT, new_kb, new_vb, *, n_b, t_q, q_pos0):
    assert B_HEADS * t_q == LANES
    p_len = kb.shape[1]
    eye = jnp.eye(B_HEADS, dtype=qbT.dtype)
    q4 = qbT.reshape(B_HEADS, HD, n_b, t_q).transpose(2, 0, 1, 3)
    q_heads = q4[:, :, :, None, :] * eye[None, :, None, :, None]
    q_heads = q_heads.reshape(n_b, B_HEADS * HD, LANES)
    o = pl.pallas_call(
        functools.partial(_sb_step_kernel, t_q=t_q, q_pos0=q_pos0), grid=(n_b,),
        in_specs=[
            pl.BlockSpec((None, 512, LANES), lambda b: (b, 0, 0)),
            pl.BlockSpec((None, p_len, 512), lambda b: (b, 0, 0)),
            pl.BlockSpec((None, 512, p_len), lambda b: (b, 0, 0)),
            pl.BlockSpec((t_q, 512), lambda b: (b, 0)), pl.BlockSpec((t_q, 512), lambda b: (b, 0)),
        ],
        out_specs=pl.BlockSpec((None, 512, LANES), lambda b: (b, 0, 0)),
        out_shape=jax.ShapeDtypeStruct((n_b, 512, LANES), jnp.float32),
        compiler_params=_cparams(("parallel",)), name="sb_step",
    )(q_heads, kb, vbT, new_kb, new_vb)
    o = jnp.diagonal(o.reshape(n_b, B_HEADS, HD, B_HEADS, t_q), axis1=1, axis2=3)
    return o.transpose(3, 1, 0, 2).reshape(1, B_HEADS * HD, n_b * t_q).astype(MXU_DTYPE)


def _top_values(s, n):
    rows = s.shape[0]
    rid = lax.broadcasted_iota(jnp.int32, s.shape, 0).astype(jnp.float32)
    vals = []
    for _ in range(n):
        m = jnp.max(s, axis=0, keepdims=True)
        first = jnp.min(jnp.where(s == m, rid, float(rows)), axis=0, keepdims=True)
        vals.append(m)
        s = jnp.where(rid == first, -jnp.inf, s)
    return vals


def _top_values_untied(s, n):
    n_inf = _col_reduce(jnp.where(s == -jnp.inf, 1.0, 0.0), jnp.sum)
    vals = []
    for _ in range(n):
        m = jnp.max(s, axis=0, keepdims=True)
        vals.append(m)
        s = jnp.where(s == m, -jnp.inf, s)
    extra = _col_reduce(jnp.where(s == -jnp.inf, 1.0, 0.0), jnp.sum) - n_inf - float(n)
    return vals, extra


def _expert_score_stats(s1, s2, top_values):
    r1, r2 = top_values(s1, P_TOPK + 1), top_values(s2, P_TOPK + 1)
    a1, a2 = (r1[0], r2[0]) if isinstance(r1, tuple) else (r1, r2)
    rc = top_values(_candidate_sums(a1, a2), P_TOPK + 1)
    top = rc[0] if isinstance(rc, tuple) else rc
    pad = [jnp.full_like(top[0], -jnp.inf)] * (3 * SUBLANES - (P_TOPK + 1))
    top = jnp.concatenate(top + pad, axis=0)
    if isinstance(rc, tuple):
        return top, jnp.maximum(jnp.maximum(r1[1], r2[1]), rc[1])
    return top


def _candidate_sums(a1, a2):
    tm = a1[0].shape[1]
    ninf = jnp.full((1, tm), -jnp.inf, jnp.float32)
    pad = [ninf] * (3 * SUBLANES - (P_TOPK + 1))
    a1m = jnp.concatenate(a1 + pad, axis=0)
    a2m = jnp.concatenate(a2 + pad, axis=0)
    r8 = lax.broadcasted_iota(jnp.int32, (SUBLANES, tm), 0)
    pieces = [a1[0] + a2m, a1[1] + a2m[0:SUBLANES]]
    for i in range(2, SUBLANES):
        pieces.append(jnp.where(r8 < (P_TOPK + 1) // (i + 1), a1[i] + a2m[0:SUBLANES], -jnp.inf))
    pieces.append(a1m[SUBLANES:] + a2[0])
    return jnp.concatenate(pieces, axis=0)


def _out_proj_kernel(x_ref, oa_ref, ob_ref, sga_ref, sgb_ref, wpa_ref, wpb_ref, wo_ref, n2_ref,
                     wpq_ref, k1_ref, k2_ref,
                     x1_ref, h2_ref, e1_ref, e2_ref, gthr_ref, q_scr, *, x_token_major):
    m = sga_ref[...] * _dot(wpa_ref[...], oa_ref[...]) + sgb_ref[...] * _dot(wpb_ref[...], ob_ref[...])
    x = x_ref[...].T if x_token_major else x_ref[...]
    x1 = x + _dot(wo_ref[...], m.astype(MXU_DTYPE))
    x1_ref[...] = x1
    ms = jnp.mean(x1 * x1, axis=0, keepdims=True)
    h2 = (x1 * lax.rsqrt(ms + EPS) * n2_ref[...]).astype(MXU_DTYPE)
    h2_ref[...] = h2
    q_scr[...] = _dot(wpq_ref[...], h2).astype(MXU_DTYPE)

    def scores(hh):
        r0 = pl.multiple_of(hh * 128, 128)
        s1 = _dot(k1_ref[hh], q_scr[pl.ds(r0, 64), :])
        s2 = _dot(k2_ref[hh], q_scr[pl.ds(pl.multiple_of(r0 + 64, 64), 64), :])
        return (s1, s2) + _expert_score_stats(s1, s2, _top_values_untied)

    def finish(hh, s1, s2, top, extra):
        top = lax.cond(jnp.max(jnp.abs(extra)) > 0.0,
                       lambda: _expert_score_stats(s1, s2, _top_values), lambda: top)
        mx = top[0:1]
        z = jnp.zeros_like(mx)
        for i in range(P_TOPK):
            z = z + jnp.exp(top[i:i + 1] - mx)
        rz = 0.5 / z
        e1_ref[hh] = jnp.exp(s1 - jnp.max(s1, axis=0, keepdims=True)) * rz
        e2_ref[hh] = jnp.exp(s2 - jnp.max(s2, axis=0, keepdims=True))
        gthr_ref[pl.ds(hh, 1), :] = jnp.exp(0.5 * (top[P_TOPK - 1:P_TOPK] + top[P_TOPK:P_TOPK + 1]) - mx) * rz

    def head_group(i, carry):
        heads = [HEADS_PER_TRIP * i + j for j in range(HEADS_PER_TRIP)]
        stats = [scores(hh) for hh in heads]
        for hh, st in zip(heads, stats):
            finish(hh, *st)
        return carry

    lax.fori_loop(0, P_HEADS // HEADS_PER_TRIP, head_group, 0)


def _out_proj(x, oaT, obT, sgaT, sgbT, w_paT, w_pbT, w_oT, n2, w_pqT, k1, k2, tm, x_token_major):
    n, d = x.shape if x_token_major else x.shape[::-1]
    nt = n // tm
    tpb = oaT.shape[2] // tm
    f32, bf = jnp.float32, MXU_DTYPE
    tok = lambda i: (0, i)
    tok3 = lambda i: (0, 0, i)
    slab = lambda i: (i // tpb, 0, i % tpb)
    c2 = lambda i: (0, 0)
    c3 = lambda i: (0, 0, 0)
    out_shape = (
        jax.ShapeDtypeStruct((d, n), f32), jax.ShapeDtypeStruct((d, n), bf),
        jax.ShapeDtypeStruct((P_HEADS, N_KEYS, n), f32), jax.ShapeDtypeStruct((P_HEADS, N_KEYS, n), f32),
        jax.ShapeDtypeStruct((P_HEADS, n), f32),
    )
    out_specs = (
        pl.BlockSpec((d, tm), tok), pl.BlockSpec((d, tm), tok),
        pl.BlockSpec((P_HEADS, N_KEYS, tm), tok3), pl.BlockSpec((P_HEADS, N_KEYS, tm), tok3),
        pl.BlockSpec((P_HEADS, tm), tok),
    )
    in_specs = [
        pl.BlockSpec((tm, d), lambda i: (i, 0)) if x_token_major else pl.BlockSpec((d, tm), tok),
        pl.BlockSpec((None, 512, tm), slab), pl.BlockSpec((None, 512, tm), slab),
        pl.BlockSpec((d, tm), tok), pl.BlockSpec((d, tm), tok),
        pl.BlockSpec((d, 512), c2), pl.BlockSpec((d, 512), c2), pl.BlockSpec((d, d), c2),
        pl.BlockSpec((d, 1), c2), pl.BlockSpec((P_HEADS * 128, d), c2),
        pl.BlockSpec((P_HEADS, N_KEYS, P_HALF), c3), pl.BlockSpec((P_HEADS, N_KEYS, P_HALF), c3),
    ]
    return pl.pallas_call(
        functools.partial(_out_proj_kernel, x_token_major=x_token_major),
        grid=(nt,), in_specs=in_specs, out_specs=out_specs, out_shape=out_shape,
        scratch_shapes=[pltpu.VMEM((P_HEADS * 128, tm), bf)],
        compiler_params=_cparams(("parallel",)), name="out_proj",
    )(x, oaT, obT, sgaT, sgbT, w_paT, w_pbT, w_oT, n2, w_pqT, k1, k2)


def _twice_gelu_exact(x):
    return x * (1.0 + lax.erf(x * np.float32(math.sqrt(0.5))))


def _peer_kernel(h2_ref, u_ref, vt_ref, e1_ref, e2_ref, gthr_ref, x1_ref, o_ref, acc_ref, coef_ref, *,
                 out_token_major):
    ec = pl.program_id(1)
    tm = h2_ref.shape[1]

    @pl.when(ec == 0)
    def _():
        acc_ref[...] = jnp.zeros_like(acc_ref)

    a_all = _dot(u_ref[...], h2_ref[...])
    n_groups = TE_PEER // (SUBLANES * N_KEYS)
    for grp in range(n_groups):
        i1_group = pl.ds(pl.multiple_of((ec * n_groups + grp) * SUBLANES, SUBLANES), SUBLANES)
        for c in range(tm // LANES):
            cs = slice(c * LANES, (c + 1) * LANES)
            e1g = [e1_ref[hh, i1_group, cs] for hh in range(P_HEADS)]
            gth = [gthr_ref[hh:hh + 1, cs] for hh in range(P_HEADS)]
            for il in range(SUBLANES):
                gate = jnp.zeros((N_KEYS, LANES), jnp.float32)
                for hh in range(P_HEADS):
                    g = e1g[hh][il:il + 1, :] * e2_ref[hh, :, cs]
                    gate = gate + jnp.where(g >= gth[hh], g, 0.0)
                rows = slice((grp * SUBLANES + il) * N_KEYS, (grp * SUBLANES + il + 1) * N_KEYS)
                coef_ref[rows, cs] = (gate * _twice_gelu_exact(a_all[rows, cs])).astype(coef_ref.dtype)
    acc_ref[...] += _dot(vt_ref[...], coef_ref[...])

    @pl.when(ec == pl.num_programs(1) - 1)
    def _():
        x2 = x1_ref[...] + acc_ref[...]
        o_ref[...] = x2.T if out_token_major else x2


def _peer(h2T, u, vT, e1T, e2T, gthr, x1T, tm, out_token_major):
    d, n = x1T.shape
    n_exp = u.shape[0]
    tok = lambda i, e: (0, i)
    tok3 = lambda i, e: (0, 0, i)
    in_specs = [
        pl.BlockSpec((d, tm), tok),
        pl.BlockSpec((TE_PEER, d), lambda i, e: (e, 0)),
        pl.BlockSpec((d, TE_PEER), lambda i, e: (0, e)),
        pl.BlockSpec((P_HEADS, N_KEYS, tm), tok3), pl.BlockSpec((P_HEADS, N_KEYS, tm), tok3),
        pl.BlockSpec((P_HEADS, tm), tok),
        pl.BlockSpec((d, tm), tok),
    ]
    return pl.pallas_call(
        functools.partial(_peer_kernel, out_token_major=out_token_major), grid=(n // tm, n_exp // TE_PEER),
        in_specs=in_specs,
        out_specs=pl.BlockSpec((tm, d), lambda i, e: (i, 0)) if out_token_major else pl.BlockSpec((d, tm), tok),
        out_shape=jax.ShapeDtypeStruct((n, d) if out_token_major else (d, n), jnp.float32),
        scratch_shapes=[pltpu.VMEM((d, tm), jnp.float32), pltpu.VMEM((TE_PEER, tm), MXU_DTYPE)],
        compiler_params=_cparams(("parallel", "arbitrary")), name="peer_dense",
    )(h2T, u, vT, e1T, e2T, gthr, x1T)


def _pack_w_in(w_in):
    d = w_in.shape[0]
    cols = [w_in[:, 0:1280], w_in[:, 1280:1352], jnp.zeros((d, 56), w_in.dtype), w_in[:, 1352:]]
    return jnp.concatenate(cols, axis=1).T.astype(MXU_DTYPE)


def _rope_tables(pos):
    half = HD // 2
    inv = ROPE_THETA ** (-jnp.arange(half, dtype=jnp.float32) / half)
    ang = pos.astype(jnp.float32)[None, :] * inv[:, None]
    return jnp.cos(ang), jnp.sin(ang)


def _token_tile(n, pref):
    return pref if n % pref == 0 else LANES


def _layer(x, lw, rope, n_b, t_q, q_pos0, past, topk, x_token_major, out_token_major):
    (n1, w_inT, qn, kn, ikn, w_paT, w_pbT, w_oT, n2, w_pqT, k1, k2, u, vT) = lw
    n = n_b * t_q
    tm = _token_tile(n, TM_PROJ)
    n_bo = n_b if past is None else 1
    (qaT, qiT, wiT, qbT, vaT, vbT, sgaT, sgbT, ka, va, ki, kb, vb) = _in_proj(
        x, n1, w_inT, qn, kn, ikn, rope[0], rope[1], tm, n_bo, x_token_major)
    new = (ka, va, ki, kb, vb)
    if past is None:
        s_valid = t_q
        k_a = ka.reshape(n_b, t_q, 128)
        k_i = ki.reshape(n_b, t_q, 64)
        k_b = kb.reshape(n_b, t_q, 512)
        v_aT, v_bT = vaT, vbT
        new_a = None
    else:
        k_a, v_aT, k_i, k_b, v_bT = past
        s_valid = k_a.shape[1] + t_q
        new_a = (ka, va, ki)
    oaT = _dsa(qaT, qiT, wiT, k_a, v_aT, k_i, new_a, n_b=n_b, t_q=t_q, q_pos0=q_pos0, s_valid=s_valid, topk=topk)
    if past is None:
        obT = _sb(qbT, k_b, v_bT, n_b=n_b, t_q=t_q)
    else:
        obT = _sb_step(qbT, k_b, v_bT, kb, vb, n_b=n_b, t_q=t_q, q_pos0=q_pos0)
    x1T, h2T, e1T, e2T, gthr = _out_proj(
        x, oaT, obT, sgaT, sgbT, w_paT, w_pbT, w_oT, n2, w_pqT, k1, k2, tm, x_token_major)
    x2 = _peer(h2T, u, vT, e1T, e2T, gthr, x1T, _token_tile(n, TM_PEER), out_token_major)
    return x2, new


def kernel(x_prompt, x_sample, cache_a_k, cache_a_v, cache_idx_k, cache_b_k, cache_b_v, norm1, w_in, q_norm_a, k_norm_a, idx_k_norm, w_pa, w_pb, w_o, norm2, peer_wq, peer_k1, peer_k2, peer_u, peer_v):
    n_bp, t_p, d = x_prompt.shape
    n_bs, t_s, _ = x_sample.shape
    depth = w_in.shape[0]
    past_len = cache_a_k.shape[2]
    assert t_p % LANES == 0 and n_bs * t_s == LANES and past_len % LANES == 0 and t_s % SUBLANES == 0
    assert peer_u.shape[1] % TE_PEER == 0
    topk_p = min(TOPK_MAX, t_p // 4)
    topk_s = min(TOPK_MAX, (past_len + t_s) // 4)
    bf = MXU_DTYPE
    col = lambda g: g.reshape(-1, 1)

    rope_p = _rope_tables(jnp.arange(t_p, dtype=jnp.int32))
    rope_s = _rope_tables(jnp.tile(past_len + jnp.arange(t_s, dtype=jnp.int32), n_bs))

    xp = x_prompt.reshape(n_bp * t_p, d)
    xs = x_sample.reshape(n_bs * t_s, d)
    st_p, st_s = [], []
    for l in range(depth):
        first, last = l == 0, l == depth - 1
        lw = (col(norm1[l]), _pack_w_in(w_in[l]), col(q_norm_a[l]), col(k_norm_a[l]), col(idx_k_norm[l]),
              w_pa[l].T.astype(bf), w_pb[l].T.astype(bf), w_o[l].T.astype(bf), col(norm2[l]),
              peer_wq[l].T.astype(bf), peer_k1[l].astype(bf), peer_k2[l].astype(bf),
              peer_u[l].astype(bf), peer_v[l].T.astype(bf))
        past = (cache_a_k[l].reshape(n_bs, past_len, -1),
                cache_a_v[l].reshape(n_bs, past_len, -1).transpose(0, 2, 1).astype(bf),
                cache_idx_k[l],
                cache_b_k[l].reshape(n_bs, past_len, -1),
                cache_b_v[l].reshape(n_bs, past_len, -1).transpose(0, 2, 1).astype(bf))
        xp, sp = _layer(xp, lw, rope_p, n_bp, t_p, 0, None, topk_p, first, last)
        xs, ss = _layer(xs, lw, rope_s, n_bs, t_s, past_len, past, topk_s, first, last)
        st_p.append(sp)
        st_s.append(ss)

    def stack(sts, j, n_b, t, shape):
        return jnp.stack([s[j] for s in sts]).reshape((depth, n_b, t) + shape)

    outs = [xp.reshape(n_bp, t_p, d), xs.reshape(n_bs, t_s, d)]
    shapes = ((A_KV_HEADS, HD), (A_KV_HEADS, HD), (IDX_DIM,), (B_HEADS, HD), (B_HEADS, HD))
    for sts, n_b, t in ((st_p, n_bp, t_p), (st_s, n_bs, t_s)):
        for j, shp in enumerate(shapes):
            outs.append(stack(sts, j, n_b, t, shp))
    return tuple(outs)
```

```python
import functools
import math

import jax
import jax.numpy as jnp
import numpy as np
from jax import lax
from jax.experimental import pallas as pl
from jax.experimental.pallas import tpu as pltpu

CHUNK = 64
EPS = 1e-6
ROPE_THETA = 10000.0
NEG = -1e30
A_HEADS, A_KV_HEADS, HD = 8, 2, 64
A_GROUP = A_HEADS // A_KV_HEADS
IDX_HEADS, IDX_DIM = 8, 64
TOPK_MAX = 256
B_HEADS = 8
P_HEADS, N_KEYS, P_HALF, P_TOPK = 8, 128, 64, 16

LANES = 128
SUBLANES = 8
VMEM_LIMIT = 56 * 1024 * 1024
TM_PROJ = 256
TM_PEER = 1024
TE_PEER = SUBLANES * N_KEYS
KEY_CHUNK = 512
HEADS_PER_TRIP = 4
N_EXTENT_CLASSES = 16

MXU_DTYPE = jnp.bfloat16

SEG_QA, SEG_KA, SEG_VA, SEG_QI, SEG_KIWI = 0, 512, 640, 768, 1280
SEG_QB, SEG_KB, SEG_VB, SEG_GA, SEG_GB, N_IN_PAD = 1408, 1920, 2432, 2944, 3968, 4992


def _cparams(sem):
    return pltpu.CompilerParams(dimension_semantics=sem, vmem_limit_bytes=VMEM_LIMIT)


def _dot(a, b):
    return jnp.dot(a, b, preferred_element_type=jnp.float32)


def _col_reduce(x, op):
    rows = x.shape[0]
    if rows % 64 == 0 and rows > 64:
        x = op(x.reshape(rows // 64, 64, x.shape[1]), axis=0)
    return op(x, axis=0, keepdims=True)


def _rms_rope_head(blk, gain, cos, sin, scale):
    if gain is not None:
        ms = jnp.mean(blk * blk, axis=0, keepdims=True)
        blk = blk * lax.rsqrt(ms + EPS) * gain
    x1, x2 = blk[:32], blk[32:]
    o1 = x1 * cos - x2 * sin
    o2 = x2 * cos + x1 * sin
    if scale != 1.0:
        o1, o2 = o1 * scale, o2 * scale
    return o1, o2


def _in_proj_kernel(x_ref, n1_ref, w_ref, qn_ref, kn_ref, ikn_ref, cos_ref, sin_ref,
                    qa_ref, qi_ref, wi_ref, qb_ref, vat_ref, vbt_ref, sga_ref, sgb_ref,
                    ka_ref, va_ref, ki_ref, kb_ref, vb_ref, *, x_token_major):
    x = x_ref[...].T if x_token_major else x_ref[...]
    ms = jnp.mean(x * x, axis=0, keepdims=True)
    h = (x * lax.rsqrt(ms + EPS) * n1_ref[...]).astype(MXU_DTYPE)
    cos, sin = cos_ref[...], sin_ref[...]
    tm = x.shape[1]

    def seg(start, size):
        return _dot(w_ref[start:start + size, :], h)

    y = seg(SEG_QA, 512)
    for hh in range(A_HEADS):
        o1, o2 = _rms_rope_head(y[hh * 64:(hh + 1) * 64], qn_ref[...], cos, sin, HD ** -0.5)
        qa_ref[hh * 64:hh * 64 + 32, :] = o1.astype(qa_ref.dtype)
        qa_ref[hh * 64 + 32:(hh + 1) * 64, :] = o2.astype(qa_ref.dtype)
    y = seg(SEG_KA, 256)
    parts = []
    for hh in range(A_KV_HEADS):
        o1, o2 = _rms_rope_head(y[hh * 64:(hh + 1) * 64], kn_ref[...], cos, sin, 1.0)
        parts += [o1, o2]
    ka_ref[...] = jnp.concatenate(parts, axis=0).T
    va_t = y[128:256]
    va_ref[...] = va_t.T
    vat_ref[...] = va_t.astype(vat_ref.dtype)
    y = seg(SEG_QI, 512)
    for hh in range(IDX_HEADS):
        o1, o2 = _rms_rope_head(y[hh * 64:(hh + 1) * 64], None, cos, sin, IDX_DIM ** -0.5)
        qi_ref[hh * 64:hh * 64 + 32, :] = o1.astype(qi_ref.dtype)
        qi_ref[hh * 64 + 32:(hh + 1) * 64, :] = o2.astype(qi_ref.dtype)
    y = seg(SEG_KIWI, 128)
    o1, o2 = _rms_rope_head(y[0:64], ikn_ref[...], cos, sin, 1.0)
    ki_t = jnp.concatenate([o1, o2, jnp.zeros((64, tm), jnp.float32)], axis=0)
    ki_ref[...] = ki_t.T[:, :64]
    wi_ref[...] = y[64:72] * (IDX_HEADS ** -0.5)
    qb_ref[...] = (seg(SEG_QB, 512) * (HD ** -0.5)).astype(qb_ref.dtype)
    kb_ref[...] = seg(SEG_KB, 512).T
    y = seg(SEG_VB, 512)
    vb_ref[...] = y.T
    vbt_ref[...] = y.astype(vbt_ref.dtype)
    sga_ref[...] = jax.nn.sigmoid(seg(SEG_GA, 1024))
    sgb_ref[...] = jax.nn.sigmoid(seg(SEG_GB, 1024))


def _in_proj(x, n1, w_inT, qn, kn, ikn, cosT, sinT, tm, n_bo, x_token_major):
    n, d = x.shape if x_token_major else x.shape[::-1]
    nt = n // tm
    t_o = n // n_bo
    tpb = t_o // tm
    pos_blocks = cosT.shape[1] // tm
    f32, bf = jnp.float32, MXU_DTYPE
    tok = lambda i: (0, i)
    row = lambda i: (i, 0)
    const = lambda i: (0, 0)
    slab = lambda i: (i // tpb, 0, i % tpb)
    fm = lambda f, dt: jax.ShapeDtypeStruct((n_bo, f, t_o), dt)
    out_shape = (
        fm(512, bf), fm(512, bf), fm(8, f32), fm(512, bf), fm(128, bf), fm(512, bf),
        jax.ShapeDtypeStruct((1024, n), f32),
        jax.ShapeDtypeStruct((1024, n), f32),
        jax.ShapeDtypeStruct((n, 128), f32),
        jax.ShapeDtypeStruct((n, 128), f32),
        jax.ShapeDtypeStruct((n, 64), f32),
        jax.ShapeDtypeStruct((n, 512), f32),
        jax.ShapeDtypeStruct((n, 512), f32),
    )
    out_specs = (
        pl.BlockSpec((None, 512, tm), slab), pl.BlockSpec((None, 512, tm), slab), pl.BlockSpec((None, 8, tm), slab),
        pl.BlockSpec((None, 512, tm), slab), pl.BlockSpec((None, 128, tm), slab), pl.BlockSpec((None, 512, tm), slab),
        pl.BlockSpec((1024, tm), tok), pl.BlockSpec((1024, tm), tok),
        pl.BlockSpec((tm, 128), row), pl.BlockSpec((tm, 128), row), pl.BlockSpec((tm, 64), row),
        pl.BlockSpec((tm, 512), row), pl.BlockSpec((tm, 512), row),
    )
    in_specs = [
        pl.BlockSpec((tm, d), row) if x_token_major else pl.BlockSpec((d, tm), tok),
        pl.BlockSpec((d, 1), const), pl.BlockSpec((N_IN_PAD, d), const),
        pl.BlockSpec((64, 1), const), pl.BlockSpec((64, 1), const), pl.BlockSpec((64, 1), const),
        pl.BlockSpec((32, tm), lambda i: (0, i % pos_blocks)),
        pl.BlockSpec((32, tm), lambda i: (0, i % pos_blocks)),
    ]
    return pl.pallas_call(
        functools.partial(_in_proj_kernel, x_token_major=x_token_major),
        grid=(nt,), in_specs=in_specs, out_specs=out_specs, out_shape=out_shape,
        compiler_params=_cparams(("parallel",)), name="in_proj",
    )(x, n1, w_inT, qn, kn, ikn, cosT, sinT)


def _extent_classes(t_q, s_total):
    if t_q % LANES:
        return [(0, 1, s_total)]
    nq = t_q // LANES
    cq = max(1, nq // N_EXTENT_CLASSES)
    assert nq % cq == 0
    return [(c * cq, cq, (c + 1) * cq * LANES) for c in range(nq // cq)]


def _query_positions(b, qblk, *, t_q, q_pos0, q_blk0):
    lane = lax.broadcasted_iota(jnp.int32, (1, LANES), 1)
    if t_q % LANES == 0:
        return q_pos0 + (q_blk0 + qblk) * LANES + lane, None
    return q_pos0 + lane % t_q, (lane // t_q) == b


def _store_queries(o_ref, rows, val, valid, b):
    if valid is None:
        o_ref[rows, :] = val
    else:
        @pl.when(b == 0)
        def _():
            o_ref[rows, :] = jnp.zeros(val.shape, o_ref.dtype)
        o_ref[rows, :] = jnp.where(valid, val, o_ref[rows, :])


def _pad_rows(x):
    return jnp.concatenate([x, jnp.zeros((LANES - x.shape[0], x.shape[1]), x.dtype)], axis=0)


def _keys_token_major(cache_ref, new_ref):
    if new_ref is None:
        return cache_ref[...]
    return jnp.concatenate([cache_ref[...], _pad_rows(new_ref[...])], axis=0)


def _values_feature_major(cache_t_ref, new_ref):
    if new_ref is None:
        return cache_t_ref[...]
    return jnp.concatenate([cache_t_ref[...], _pad_rows(new_ref[...]).T.astype(MXU_DTYPE)], axis=1)


def _float_order_key(x):
    bits = lax.bitcast_convert_type(x, jnp.int32)
    key = jnp.where(bits < 0, bits ^ jnp.int32(0x7FFFFFFF), bits)
    return jnp.where(x == 0.0, jnp.int32(0), key)


def _count(mask):
    return _col_reduce(jnp.where(mask, 1.0, 0.0), jnp.sum)


def _dsa_kernel(qa_ref, qi_ref, wi_ref, k_ref, vt_ref, ki_ref, *rest, t_q, q_pos0, q_blk0, s_valid, topk):
    b, qblk = pl.program_id(0), pl.program_id(1)
    (kn_ref, vn_ref, kin_ref), o_ref = (rest[:3] if len(rest) == 4 else (None, None, None)), rest[-1]
    k_all = _keys_token_major(k_ref, kn_ref)
    ki_all = _keys_token_major(ki_ref, kin_ref)
    vt_all = _values_feature_major(vt_ref, vn_ref)
    s_pad = k_all.shape[0]
    qpos, valid = _query_positions(b, qblk, t_q=t_q, q_pos0=q_pos0, q_blk0=q_blk0)
    kpos = lax.broadcasted_iota(jnp.int32, (s_pad, LANES), 0)
    adm = ((kpos // CHUNK) <= (qpos // CHUNK)) & (kpos < s_valid)

    qcat = jnp.concatenate([qi_ref[hh * 64:(hh + 1) * 64, :] for hh in range(IDX_HEADS)], axis=1)
    parts = []
    for r0 in range(0, s_pad, KEY_CHUNK):
        r1 = min(s_pad, r0 + KEY_CHUNK)
        z = _dot(ki_all[r0:r1, :].astype(MXU_DTYPE), qcat)
        acc = jnp.maximum(z[:, :LANES], 0.0) * wi_ref[0:1, :]
        for hh in range(1, IDX_HEADS):
            acc = acc + jnp.maximum(z[:, hh * LANES:(hh + 1) * LANES], 0.0) * wi_ref[hh:hh + 1, :]
        parts.append(acc)
    isc = jnp.concatenate(parts, axis=0) if len(parts) > 1 else parts[0]
    key = _float_order_key(jnp.where(adm, isc, NEG))

    kf = float(topk)
    int_min = jnp.int32(-2 ** 31)
    base = jnp.where(_count(key >= 0) >= kf, jnp.int32(0), int_min)

    def bit_step(i, base):
        cand = base + jnp.left_shift(jnp.int32(1), jnp.int32(30) - i)
        return jnp.where(_count(key >= cand) >= kf, cand, base)

    thr = lax.fori_loop(0, 31, bit_step, base)
    gt = key > thr
    tied = key == thr
    need = kf - _count(gt)
    row = lax.broadcasted_iota(jnp.int32, (LANES, LANES), 0)
    col = lax.broadcasted_iota(jnp.int32, (LANES, LANES), 1)
    tril = jnp.where(col <= row, 1.0, 0.0).astype(MXU_DTYPE)
    tied01 = jnp.where(tied, 1.0, 0.0).astype(MXU_DTYPE)
    carry = jnp.zeros((1, LANES), jnp.float32)
    ranks = []
    for c in range(s_pad // LANES):
        rank = _dot(tril, tied01[c * LANES:(c + 1) * LANES]) + carry
        ranks.append(rank)
        carry = rank[LANES - 1:LANES, :]
    rank = jnp.concatenate(ranks, axis=0) if len(ranks) > 1 else ranks[0]
    sel = gt | (tied & (rank <= need))
    bias = jnp.where(sel & adm, 0.0, NEG)

    kb16 = k_all.astype(MXU_DTYPE)
    zeros64 = jnp.zeros((64, LANES), MXU_DTYPE)
    for n in range(A_KV_HEADS):
        qs = []
        for g in range(A_GROUP):
            hh = n * A_GROUP + g
            qh = qa_ref[hh * 64:(hh + 1) * 64, :]
            qs.append(jnp.concatenate([qh, zeros64] if n == 0 else [zeros64, qh], axis=0))
        s = _dot(kb16, jnp.concatenate(qs, axis=1))
        ps, ls = [], []
        for g in range(A_GROUP):
            sg = s[:, g * LANES:(g + 1) * LANES] + bias
            p = jnp.exp(sg - _col_reduce(sg, jnp.max))
            ls.append(_col_reduce(p, jnp.sum))
            ps.append(p.astype(MXU_DTYPE))
        o = _dot(vt_all[n * 64:(n + 1) * 64, :], jnp.concatenate(ps, axis=1))
        for g in range(A_GROUP):
            hh = n * A_GROUP + g
            og = (o[:, g * LANES:(g + 1) * LANES] / ls[g]).astype(o_ref.dtype)
            _store_queries(o_ref, slice(hh * 64, (hh + 1) * 64), og, valid, b)


def _dsa(qaT, qiT, wiT, k, vT, ki, new, *, n_b, t_q, q_pos0, s_valid, topk):
    shared = t_q % LANES != 0
    s_cache = k.shape[1]
    outs = []
    for q_blk0, n_qblk, s_used in _extent_classes(t_q, s_cache):
        qmap = lambda b, q, q0=q_blk0: (0 if shared else b, 0, q0 + q)
        omap = lambda b, q: (0 if shared else b, 0, q)
        kern = functools.partial(_dsa_kernel, t_q=t_q, q_pos0=q_pos0, q_blk0=q_blk0, s_valid=s_valid, topk=topk)
        in_specs = [
            pl.BlockSpec((None, 512, LANES), qmap), pl.BlockSpec((None, 512, LANES), qmap),
            pl.BlockSpec((None, 8, LANES), qmap),
            pl.BlockSpec((None, s_used, 128), lambda b, q: (b, 0, 0)),
            pl.BlockSpec((None, 128, s_used), lambda b, q: (b, 0, 0)),
            pl.BlockSpec((None, s_used, 64), lambda b, q: (b, 0, 0)),
        ]
        args = [qaT, qiT, wiT, k, vT, ki]
        if new is not None:
            in_specs += [pl.BlockSpec((t_q, f), lambda b, q: (b, 0)) for f in (128, 128, 64)]
            args += list(new)
        outs.append(pl.pallas_call(
            kern, grid=(n_b, n_qblk), in_specs=in_specs,
            out_specs=pl.BlockSpec((None, 512, LANES), omap),
            out_shape=jax.ShapeDtypeStruct((qaT.shape[0], 512, n_qblk * LANES), MXU_DTYPE),
            compiler_params=_cparams(("arbitrary", "arbitrary")), name="dsa_attn",
        )(*args))
    return outs[0] if len(outs) == 1 else jnp.concatenate(outs, axis=2)


def _split3(x):
    top_bits = jnp.int32(-65536)
    trunc = lambda v: lax.bitcast_convert_type(lax.bitcast_convert_type(v, jnp.int32) & top_bits, jnp.float32)
    hi = trunc(x)
    r1 = x - hi
    mid = trunc(r1)
    return hi.astype(MXU_DTYPE), mid.astype(MXU_DTYPE), (r1 - mid).astype(MXU_DTYPE)


def _sb_weights(z_all, qpos, first_qpos):
    n_blk = z_all.shape[0] // LANES
    row = lax.broadcasted_iota(jnp.int32, (LANES, LANES), 0)
    col = lax.broadcasted_iota(jnp.int32, (LANES, LANES), 1)
    tri = jnp.where(col > row, 1.0, 0.0).astype(MXU_DTYPE)
    blocks, tots = [], []
    for c in range(n_blk):
        z = z_all[c * LANES:(c + 1) * LANES]
        before = None if (c + 1) * LANES <= first_qpos else (c * LANES + row) < qpos
        ls = jnp.minimum(z, 0.0) - jnp.log(1.0 + jnp.exp(-jnp.abs(z)))
        lr = ls - z
        if before is not None:
            lr = jnp.where(before, lr, 0.0)
        aft = _dot(tri, jnp.concatenate(_split3(lr), axis=1))
        aft = aft[:, :LANES] + aft[:, LANES:2 * LANES] + aft[:, 2 * LANES:]
        blocks.append((ls + aft, before))
        tots.append(jnp.sum(lr, axis=0, keepdims=True))
    rest_lr = jnp.zeros((1, LANES), jnp.float32)
    atts = [None] * n_blk
    for c in reversed(range(n_blk)):
        arg, before = blocks[c]
        att = jnp.exp(arg + rest_lr)
        if before is not None:
            att = jnp.where(before, att, 0.0)
        atts[c] = att.astype(MXU_DTYPE)
        rest_lr = rest_lr + tots[c]
    return jnp.concatenate(atts, axis=0) if n_blk > 1 else atts[0]


def _sb_kernel(q_ref, k_ref, vt_ref, o_ref, *, q_blk0):
    qblk = pl.program_id(2)
    k16 = k_ref[...].astype(MXU_DTYPE)
    qpos, _ = _query_positions(0, qblk, t_q=LANES, q_pos0=0, q_blk0=q_blk0)
    zeros64 = jnp.zeros((64, LANES), MXU_DTYPE)
    for hh in range(2):
        qh = q_ref[hh * 64:(hh + 1) * 64, :]
        z_all = _dot(k16, jnp.concatenate([qh, zeros64] if hh == 0 else [zeros64, qh], axis=0))
        att = _sb_weights(z_all, qpos, q_blk0 * LANES)
        o_ref[hh * 64:(hh + 1) * 64, :] = _dot(vt_ref[hh * 64:(hh + 1) * 64, :], att).astype(o_ref.dtype)


def _sb_step_kernel(q_ref, k_ref, vt_ref, kn_ref, vn_ref, o_ref, *, t_q, q_pos0):
    k16 = _keys_token_major(k_ref, kn_ref).astype(MXU_DTYPE)
    vt_all = _values_feature_major(vt_ref, vn_ref)
    lane = lax.broadcasted_iota(jnp.int32, (1, LANES), 1)
    att = _sb_weights(_dot(k16, q_ref[...]), q_pos0 + lane % t_q, q_pos0)
    o_ref[...] = _dot(vt_all, att).astype(o_ref.dtype)


def _sb(qbT, kb, vbT, *, n_b, t_q):
    outs = []
    for q_blk0, n_qblk, s_used in _extent_classes(t_q, kb.shape[1]):
        outs.append(pl.pallas_call(
            functools.partial(_sb_kernel, q_blk0=q_blk0), grid=(B_HEADS // 2, n_b, n_qblk),
            in_specs=[
                pl.BlockSpec((None, 128, LANES), lambda hp, b, q, q0=q_blk0: (b, hp, q0 + q)),
                pl.BlockSpec((None, s_used, 128), lambda hp, b, q: (b, 0, hp)),
                pl.BlockSpec((None, 128, s_used), lambda hp, b, q: (b, hp, 0)),
            ],
            out_specs=pl.BlockSpec((None, 128, LANES), lambda hp, b, q: (b, hp, q)),
            out_shape=jax.ShapeDtypeStruct((n_b, 512, n_qblk * LANES), MXU_DTYPE),
            compiler_params=_cparams(("parallel", "parallel", "parallel")), name="sb_attn",
        )(qbT, kb, vbT))
    return outs[0] if len(outs) == 1 else jnp.concatenate(outs, axis=2)


def _sb_step(qbT, kb, vbT, new_kb, new_vb, *, n_b, t_q, q_pos0):
    assert B_HEADS * t_q == LANES
    p_len = kb.shape[1]
    eye = jnp.eye(B_HEADS, dtype=qbT.dtype)
    q4 = qbT.reshape(B_HEADS, HD, n_b, t_q).transpose(2, 0, 1, 3)
    q_heads = q4[:, :, :, None, :] * eye[None, :, None, :, None]
    q_heads = q_heads.reshape(n_b, B_HEADS * HD, LANES)
    o = pl.pallas_call(
        functools.partial(_sb_step_kernel, t_q=t_q, q_pos0=q_pos0), grid=(n_b,),
        in_specs=[
            pl.BlockSpec((None, 512, LANES), lambda b: (b, 0, 0)),
            pl.BlockSpec((None, p_len, 512), lambda b: (b, 0, 0)),
            pl.BlockSpec((None, 512, p_len), lambda b: (b, 0, 0)),
            pl.BlockSpec((t_q, 512), lambda b: (b, 0)), pl.BlockSpec((t_q, 512), lambda b: (b, 0)),
        ],
        out_specs=pl.BlockSpec((None, 512, LANES), lambda b: (b, 0, 0)),
        out_shape=jax.ShapeDtypeStruct((n_b, 512, LANES), jnp.float32),
        compiler_params=_cparams(("parallel",)), name="sb_step",
    )(q_heads, kb, vbT, new_kb, new_vb)
    o = jnp.diagonal(o.reshape(n_b, B_HEADS, HD, B_HEADS, t_q), axis1=1, axis2=3)
    return o.transpose(3, 1, 0, 2).reshape(1, B_HEADS * HD, n_b * t_q).astype(MXU_DTYPE)


def _top_values(s, n):
    rows = s.shape[0]
    rid = lax.broadcasted_iota(jnp.int32, s.shape, 0).astype(jnp.float32)
    vals = []
    for _ in range(n):
        m = jnp.max(s, axis=0, keepdims=True)
        first = jnp.min(jnp.where(s == m, rid, float(rows)), axis=0, keepdims=True)
        vals.append(m)
        s = jnp.where(rid == first, -jnp.inf, s)
    return vals


def _top_values_untied(s, n):
    n_inf = _col_reduce(jnp.where(s == -jnp.inf, 1.0, 0.0), jnp.sum)
    vals = []
    for _ in range(n):
        m = jnp.max(s, axis=0, keepdims=True)
        vals.append(m)
        s = jnp.where(s == m, -jnp.inf, s)
    extra = _col_reduce(jnp.where(s == -jnp.inf, 1.0, 0.0), jnp.sum) - n_inf - float(n)
    return vals, extra


def _expert_score_stats(s1, s2, top_values):
    r1, r2 = top_values(s1, P_TOPK + 1), top_values(s2, P_TOPK + 1)
    a1, a2 = (r1[0], r2[0]) if isinstance(r1, tuple) else (r1, r2)
    rc = top_values(_candidate_sums(a1, a2), P_TOPK + 1)
    top = rc[0] if isinstance(rc, tuple) else rc
    pad = [jnp.full_like(top[0], -jnp.inf)] * (3 * SUBLANES - (P_TOPK + 1))
    top = jnp.concatenate(top + pad, axis=0)
    if isinstance(rc, tuple):
        return top, jnp.maximum(jnp.maximum(r1[1], r2[1]), rc[1])
    return top


def _candidate_sums(a1, a2):
    tm = a1[0].shape[1]
    ninf = jnp.full((1, tm), -jnp.inf, jnp.float32)
    pad = [ninf] * (3 * SUBLANES - (P_TOPK + 1))
    a1m = jnp.concatenate(a1 + pad, axis=0)
    a2m = jnp.concatenate(a2 + pad, axis=0)
    r8 = lax.broadcasted_iota(jnp.int32, (SUBLANES, tm), 0)
    pieces = [a1[0] + a2m, a1[1] + a2m[0:SUBLANES]]
    for i in range(2, SUBLANES):
        pieces.append(jnp.where(r8 < (P_TOPK + 1) // (i + 1), a1[i] + a2m[0:SUBLANES], -jnp.inf))
    pieces.append(a1m[SUBLANES:] + a2[0])
    return jnp.concatenate(pieces, axis=0)


def _out_proj_kernel(x_ref, oa_ref, ob_ref, sga_ref, sgb_ref, wpa_ref, wpb_ref, wo_ref, n2_ref,
                     wpq_ref, k1_ref, k2_ref,
                     x1_ref, h2_ref, e1_ref, e2_ref, gthr_ref, q_scr, *, x_token_major):
    m = sga_ref[...] * _dot(wpa_ref[...], oa_ref[...]) + sgb_ref[...] * _dot(wpb_ref[...], ob_ref[...])
    x = x_ref[...].T if x_token_major else x_ref[...]
    x1 = x + _dot(wo_ref[...], m.astype(MXU_DTYPE))
    x1_ref[...] = x1
    ms = jnp.mean(x1 * x1, axis=0, keepdims=True)
    h2 = (x1 * lax.rsqrt(ms + EPS) * n2_ref[...]).astype(MXU_DTYPE)
    h2_ref[...] = h2
    q_scr[...] = _dot(wpq_ref[...], h2).astype(MXU_DTYPE)

    def scores(hh):
        r0 = pl.multiple_of(hh * 128, 128)
        s1 = _dot(k1_ref[hh], q_scr[pl.ds(r0, 64), :])
        s2 = _dot(k2_ref[hh], q_scr[pl.ds(pl.multiple_of(r0 + 64, 64), 64), :])
        return (s1, s2) + _expert_score_stats(s1, s2, _top_values_untied)

    def finish(hh, s1, s2, top, extra):
        top = lax.cond(jnp.max(jnp.abs(extra)) > 0.0,
                       lambda: _expert_score_stats(s1, s2, _top_values), lambda: top)
        mx = top[0:1]
        z = jnp.zeros_like(mx)
        for i in range(P_TOPK):
            z = z + jnp.exp(top[i:i + 1] - mx)
        rz = 0.5 / z
        e1_ref[hh] = jnp.exp(s1 - jnp.max(s1, axis=0, keepdims=True)) * rz
        e2_ref[hh] = jnp.exp(s2 - jnp.max(s2, axis=0, keepdims=True))
        gthr_ref[pl.ds(hh, 1), :] = jnp.exp(0.5 * (top[P_TOPK - 1:P_TOPK] + top[P_TOPK:P_TOPK + 1]) - mx) * rz

    def head_group(i, carry):
        heads = [HEADS_PER_TRIP * i + j for j in range(HEADS_PER_TRIP)]
        stats = [scores(hh) for hh in heads]
        for hh, st in zip(heads, stats):
            finish(hh, *st)
        return carry

    lax.fori_loop(0, P_HEADS // HEADS_PER_TRIP, head_group, 0)


def _out_proj(x, oaT, obT, sgaT, sgbT, w_paT, w_pbT, w_oT, n2, w_pqT, k1, k2, tm, x_token_major):
    n, d = x.shape if x_token_major else x.shape[::-1]
    nt = n // tm
    tpb = oaT.shape[2] // tm
    f32, bf = jnp.float32, MXU_DTYPE
    tok = lambda i: (0, i)
    tok3 = lambda i: (0, 0, i)
    slab = lambda i: (i // tpb, 0, i % tpb)
    c2 = lambda i: (0, 0)
    c3 = lambda i: (0, 0, 0)
    out_shape = (
        jax.ShapeDtypeStruct((d, n), f32), jax.ShapeDtypeStruct((d, n), bf),
        jax.ShapeDtypeStruct((P_HEADS, N_KEYS, n), f32), jax.ShapeDtypeStruct((P_HEADS, N_KEYS, n), f32),
        jax.ShapeDtypeStruct((P_HEADS, n), f32),
    )
    out_specs = (
        pl.BlockSpec((d, tm), tok), pl.BlockSpec((d, tm), tok),
        pl.BlockSpec((P_HEADS, N_KEYS, tm), tok3), pl.BlockSpec((P_HEADS, N_KEYS, tm), tok3),
        pl.BlockSpec((P_HEADS, tm), tok),
    )
    in_specs = [
        pl.BlockSpec((tm, d), lambda i: (i, 0)) if x_token_major else pl.BlockSpec((d, tm), tok),
        pl.BlockSpec((None, 512, tm), slab), pl.BlockSpec((None, 512, tm), slab),
        pl.BlockSpec((d, tm), tok), pl.BlockSpec((d, tm), tok),
        pl.BlockSpec((d, 512), c2), pl.BlockSpec((d, 512), c2), pl.BlockSpec((d, d), c2),
        pl.BlockSpec((d, 1), c2), pl.BlockSpec((P_HEADS * 128, d), c2),
        pl.BlockSpec((P_HEADS, N_KEYS, P_HALF), c3), pl.BlockSpec((P_HEADS, N_KEYS, P_HALF), c3),
    ]
    return pl.pallas_call(
        functools.partial(_out_proj_kernel, x_token_major=x_token_major),
        grid=(nt,), in_specs=in_specs, out_specs=out_specs, out_shape=out_shape,
        scratch_shapes=[pltpu.VMEM((P_HEADS * 128, tm), bf)],
        compiler_params=_cparams(("parallel",)), name="out_proj",
    )(x, oaT, obT, sgaT, sgbT, w_paT, w_pbT, w_oT, n2, w_pqT, k1, k2)


def _twice_gelu_exact(x):
    return x * (1.0 + lax.erf(x * np.float32(math.sqrt(0.5))))


def _peer_kernel(h2_ref, u_ref, vt_ref, e1_ref, e2_ref, gthr_ref, x1_ref, o_ref, acc_ref, coef_ref, *,
                 out_token_major):
    ec = pl.program_id(1)
    tm = h2_ref.shape[1]

    @pl.when(ec == 0)
    def _():
        acc_ref[...] = jnp.zeros_like(acc_ref)

    a_all = _dot(u_ref[...], h2_ref[...])
    n_groups = TE_PEER // (SUBLANES * N_KEYS)
    for grp in range(n_groups):
        i1_group = pl.ds(pl.multiple_of((ec * n_groups + grp) * SUBLANES, SUBLANES), SUBLANES)
        for c in range(tm // LANES):
            cs = slice(c * LANES, (c + 1) * LANES)
            e1g = [e1_ref[hh, i1_group, cs] for hh in range(P_HEADS)]
            gth = [gthr_ref[hh:hh + 1, cs] for hh in range(P_HEADS)]
            for il in range(SUBLANES):
                gate = jnp.zeros((N_KEYS, LANES), jnp.float32)
                for hh in range(P_HEADS):
                    g = e1g[hh][il:il + 1, :] * e2_ref[hh, :, cs]
                    gate = gate + jnp.where(g >= gth[hh], g, 0.0)
                rows = slice((grp * SUBLANES + il) * N_KEYS, (grp * SUBLANES + il + 1) * N_KEYS)
                coef_ref[rows, cs] = (gate * _twice_gelu_exact(a_all[rows, cs])).astype(coef_ref.dtype)
    acc_ref[...] += _dot(vt_ref[...], coef_ref[...])

    @pl.when(ec == pl.num_programs(1) - 1)
    def _():
        x2 = x1_ref[...] + acc_ref[...]
        o_ref[...] = x2.T if out_token_major else x2


def _peer(h2T, u, vT, e1T, e2T, gthr, x1T, tm, out_token_major):
    d, n = x1T.shape
    n_exp = u.shape[0]
    tok = lambda i, e: (0, i)
    tok3 = lambda i, e: (0, 0, i)
    once = pl.Buffered(1)
    in_specs = [
        pl.BlockSpec((d, tm), tok, pipeline_mode=once),
        pl.BlockSpec((TE_PEER, d), lambda i, e: (e, 0)),
        pl.BlockSpec((d, TE_PEER), lambda i, e: (0, e)),
        pl.BlockSpec((P_HEADS, N_KEYS, tm), tok3, pipeline_mode=once),
        pl.BlockSpec((P_HEADS, N_KEYS, tm), tok3, pipeline_mode=once),
        pl.BlockSpec((P_HEADS, tm), tok, pipeline_mode=once),
        pl.BlockSpec((d, tm), tok, pipeline_mode=once),
    ]
    return pl.pallas_call(
        functools.partial(_peer_kernel, out_token_major=out_token_major), grid=(n // tm, n_exp // TE_PEER),
        in_specs=in_specs,
        out_specs=pl.BlockSpec((tm, d), lambda i, e: (i, 0)) if out_token_major else pl.BlockSpec((d, tm), tok),
        out_shape=jax.ShapeDtypeStruct((n, d) if out_token_major else (d, n), jnp.float32),
        scratch_shapes=[pltpu.VMEM((d, tm), jnp.float32), pltpu.VMEM((TE_PEER, tm), MXU_DTYPE)],
        compiler_params=_cparams(("parallel", "arbitrary")), name="peer_dense",
    )(h2T, u, vT, e1T, e2T, gthr, x1T)


def _pack_w_in(w_in):
    d = w_in.shape[0]
    cols = [w_in[:, 0:1280], w_in[:, 1280:1352], jnp.zeros((d, 56), w_in.dtype), w_in[:, 1352:]]
    return jnp.concatenate(cols, axis=1).T.astype(MXU_DTYPE)


def _rope_tables(pos):
    half = HD // 2
    inv = ROPE_THETA ** (-jnp.arange(half, dtype=jnp.float32) / half)
    ang = pos.astype(jnp.float32)[None, :] * inv[:, None]
    return jnp.cos(ang), jnp.sin(ang)


def _token_tile(n, pref):
    return pref if n % pref == 0 else LANES


def _layer(x, lw, rope, n_b, t_q, q_pos0, past, topk, x_token_major, out_token_major):
    (n1, w_inT, qn, kn, ikn, w_paT, w_pbT, w_oT, n2, w_pqT, k1, k2, u, vT) = lw
    n = n_b * t_q
    tm = _token_tile(n, TM_PROJ)
    n_bo = n_b if past is None else 1
    (qaT, qiT, wiT, qbT, vaT, vbT, sgaT, sgbT, ka, va, ki, kb, vb) = _in_proj(
        x, n1, w_inT, qn, kn, ikn, rope[0], rope[1], tm, n_bo, x_token_major)
    new = (ka, va, ki, kb, vb)
    if past is None:
        s_valid = t_q
        k_a = ka.reshape(n_b, t_q, 128)
        k_i = ki.reshape(n_b, t_q, 64)
        k_b = kb.reshape(n_b, t_q, 512)
        v_aT, v_bT = vaT, vbT
        new_a = None
    else:
        k_a, v_aT, k_i, k_b, v_bT = past
        s_valid = k_a.shape[1] + t_q
        new_a = (ka, va, ki)
    oaT = _dsa(qaT, qiT, wiT, k_a, v_aT, k_i, new_a, n_b=n_b, t_q=t_q, q_pos0=q_pos0, s_valid=s_valid, topk=topk)
    if past is None:
        obT = _sb(qbT, k_b, v_bT, n_b=n_b, t_q=t_q)
    else:
        obT = _sb_step(qbT, k_b, v_bT, kb, vb, n_b=n_b, t_q=t_q, q_pos0=q_pos0)
    x1T, h2T, e1T, e2T, gthr = _out_proj(
        x, oaT, obT, sgaT, sgbT, w_paT, w_pbT, w_oT, n2, w_pqT, k1, k2, tm, x_token_major)
    x2 = _peer(h2T, u, vT, e1T, e2T, gthr, x1T, _token_tile(n, TM_PEER), out_token_major)
    return x2, new


def kernel(x_prompt, x_sample, cache_a_k, cache_a_v, cache_idx_k, cache_b_k, cache_b_v, norm1, w_in, q_norm_a, k_norm_a, idx_k_norm, w_pa, w_pb, w_o, norm2, peer_wq, peer_k1, peer_k2, peer_u, peer_v):
    n_bp, t_p, d = x_prompt.shape
    n_bs, t_s, _ = x_sample.shape
    depth = w_in.shape[0]
    past_len = cache_a_k.shape[2]
    assert t_p % LANES == 0 and n_bs * t_s == LANES and past_len % LANES == 0 and t_s % SUBLANES == 0
    assert peer_u.shape[1] % TE_PEER == 0
    topk_p = min(TOPK_MAX, t_p // 4)
    topk_s = min(TOPK_MAX, (past_len + t_s) // 4)
    bf = MXU_DTYPE
    col = lambda g: g.reshape(-1, 1)

    rope_p = _rope_tables(jnp.arange(t_p, dtype=jnp.int32))
    rope_s = _rope_tables(jnp.tile(past_len + jnp.arange(t_s, dtype=jnp.int32), n_bs))

    xp = x_prompt.reshape(n_bp * t_p, d)
    xs = x_sample.reshape(n_bs * t_s, d)
    st_p, st_s = [], []
    for l in range(depth):
        first, last = l == 0, l == depth - 1
        lw = (col(norm1[l]), _pack_w_in(w_in[l]), col(q_norm_a[l]), col(k_norm_a[l]), col(idx_k_norm[l]),
              w_pa[l].T.astype(bf), w_pb[l].T.astype(bf), w_o[l].T.astype(bf), col(norm2[l]),
              peer_wq[l].T.astype(bf), peer_k1[l].astype(bf), peer_k2[l].astype(bf),
              peer_u[l].astype(bf), peer_v[l].T.astype(bf))
        past = (cache_a_k[l].reshape(n_bs, past_len, -1),
                cache_a_v[l].reshape(n_bs, past_len, -1).transpose(0, 2, 1).astype(bf),
                cache_idx_k[l],
                cache_b_k[l].reshape(n_bs, past_len, -1),
                cache_b_v[l].reshape(n_bs, past_len, -1).transpose(0, 2, 1).astype(bf))
        xp, sp = _layer(xp, lw, rope_p, n_bp, t_p, 0, None, topk_p, first, last)
        xs, ss = _layer(xs, lw, rope_s, n_bs, t_s, past_len, past, topk_s, first, last)
        st_p.append(sp)
        st_s.append(ss)

    def stack(sts, j, n_b, t, shape):
        return jnp.stack([s[j] for s in sts]).reshape((depth, n_b, t) + shape)

    outs = [xp.reshape(n_bp, t_p, d), xs.reshape(n_bs, t_s, d)]
    shapes = ((A_KV_HEADS, HD), (A_KV_HEADS, HD), (IDX_DIM,), (B_HEADS, HD), (B_HEADS, HD))
    for sts, n_b, t in ((st_p, n_bp, t_p), (st_s, n_bs, t_s)):
        for j, shp in enumerate(shapes):
            outs.append(stack(sts, j, n_b, t, shp))
    return tuple(outs)
```

```python
import functools
import math

import jax
import jax.numpy as jnp
import numpy as np
from jax import lax
from jax.experimental import pallas as pl
from jax.experimental.pallas import tpu as pltpu

CHUNK = 64
EPS = 1e-6
ROPE_THETA = 10000.0
NEG = -1e30
A_HEADS, A_KV_HEADS, HD = 8, 2, 64
A_GROUP = A_HEADS // A_KV_HEADS
IDX_HEADS, IDX_DIM = 8, 64
TOPK_MAX = 256
B_HEADS = 8
P_HEADS, N_KEYS, P_HALF, P_TOPK = 8, 128, 64, 16

LANES = 128
SUBLANES = 8
VMEM_LIMIT = 56 * 1024 * 1024
TM_PROJ = 256
TM_PEER = 512
TE_PEER = 2 * SUBLANES * N_KEYS
KEY_CHUNK = 512
HEADS_PER_TRIP = 4
N_EXTENT_CLASSES = 16

MXU_DTYPE = jnp.bfloat16

SEG_QA, SEG_KA, SEG_VA, SEG_QI, SEG_KIWI = 0, 512, 640, 768, 1280
SEG_QB, SEG_KB, SEG_VB, SEG_GA, SEG_GB, N_IN_PAD = 1408, 1920, 2432, 2944, 3968, 4992


def _cparams(sem):
    return pltpu.CompilerParams(dimension_semantics=sem, vmem_limit_bytes=VMEM_LIMIT)


def _dot(a, b):
    return jnp.dot(a, b, preferred_element_type=jnp.float32)


def _col_reduce(x, op):
    rows = x.shape[0]
    if rows % 64 == 0 and rows > 64:
        x = op(x.reshape(rows // 64, 64, x.shape[1]), axis=0)
    return op(x, axis=0, keepdims=True)


def _rms_rope_head(blk, gain, cos, sin, scale):
    if gain is not None:
        ms = jnp.mean(blk * blk, axis=0, keepdims=True)
        blk = blk * lax.rsqrt(ms + EPS) * gain
    x1, x2 = blk[:32], blk[32:]
    o1 = x1 * cos - x2 * sin
    o2 = x2 * cos + x1 * sin
    if scale != 1.0:
        o1, o2 = o1 * scale, o2 * scale
    return o1, o2


def _in_proj_kernel(x_ref, n1_ref, w_ref, qn_ref, kn_ref, ikn_ref, cos_ref, sin_ref,
                    qa_ref, qi_ref, wi_ref, qb_ref, vat_ref, vbt_ref, sga_ref, sgb_ref,
                    ka_ref, va_ref, ki_ref, kb_ref, vb_ref, *, x_token_major):
    x = x_ref[...].T if x_token_major else x_ref[...]
    ms = jnp.mean(x * x, axis=0, keepdims=True)
    h = (x * lax.rsqrt(ms + EPS) * n1_ref[...]).astype(MXU_DTYPE)
    cos, sin = cos_ref[...], sin_ref[...]
    tm = x.shape[1]

    def seg(start, size):
        return _dot(w_ref[start:start + size, :], h)

    y = seg(SEG_QA, 512)
    for hh in range(A_HEADS):
        o1, o2 = _rms_rope_head(y[hh * 64:(hh + 1) * 64], qn_ref[...], cos, sin, HD ** -0.5)
        qa_ref[hh * 64:hh * 64 + 32, :] = o1.astype(qa_ref.dtype)
        qa_ref[hh * 64 + 32:(hh + 1) * 64, :] = o2.astype(qa_ref.dtype)
    y = seg(SEG_KA, 256)
    parts = []
    for hh in range(A_KV_HEADS):
        o1, o2 = _rms_rope_head(y[hh * 64:(hh + 1) * 64], kn_ref[...], cos, sin, 1.0)
        parts += [o1, o2]
    ka_ref[...] = jnp.concatenate(parts, axis=0).T
    va_t = y[128:256]
    va_ref[...] = va_t.T
    vat_ref[...] = va_t.astype(vat_ref.dtype)
    y = seg(SEG_QI, 512)
    for hh in range(IDX_HEADS):
        o1, o2 = _rms_rope_head(y[hh * 64:(hh + 1) * 64], None, cos, sin, IDX_DIM ** -0.5)
        qi_ref[hh * 64:hh * 64 + 32, :] = o1.astype(qi_ref.dtype)
        qi_ref[hh * 64 + 32:(hh + 1) * 64, :] = o2.astype(qi_ref.dtype)
    y = seg(SEG_KIWI, 128)
    o1, o2 = _rms_rope_head(y[0:64], ikn_ref[...], cos, sin, 1.0)
    ki_t = jnp.concatenate([o1, o2, jnp.zeros((64, tm), jnp.float32)], axis=0)
    ki_ref[...] = ki_t.T[:, :64]
    wi_ref[...] = y[64:72] * (IDX_HEADS ** -0.5)
    qb_ref[...] = (seg(SEG_QB, 512) * (HD ** -0.5)).astype(qb_ref.dtype)
    kb_ref[...] = seg(SEG_KB, 512).T
    y = seg(SEG_VB, 512)
    vb_ref[...] = y.T
    vbt_ref[...] = y.astype(vbt_ref.dtype)
    sga_ref[...] = jax.nn.sigmoid(seg(SEG_GA, 1024))
    sgb_ref[...] = jax.nn.sigmoid(seg(SEG_GB, 1024))


def _in_proj(x, n1, w_inT, qn, kn, ikn, cosT, sinT, tm, n_bo, x_token_major):
    n, d = x.shape if x_token_major else x.shape[::-1]
    nt = n // tm
    t_o = n // n_bo
    tpb = t_o // tm
    pos_blocks = cosT.shape[1] // tm
    f32, bf = jnp.float32, MXU_DTYPE
    tok = lambda i: (0, i)
    row = lambda i: (i, 0)
    const = lambda i: (0, 0)
    slab = lambda i: (i // tpb, 0, i % tpb)
    fm = lambda f, dt: jax.ShapeDtypeStruct((n_bo, f, t_o), dt)
    out_shape = (
        fm(512, bf), fm(512, bf), fm(8, f32), fm(512, bf), fm(128, bf), fm(512, bf),
        jax.ShapeDtypeStruct((1024, n), f32),
        jax.ShapeDtypeStruct((1024, n), f32),
        jax.ShapeDtypeStruct((n, 128), f32),
        jax.ShapeDtypeStruct((n, 128), f32),
        jax.ShapeDtypeStruct((n, 64), f32),
        jax.ShapeDtypeStruct((n, 512), f32),
        jax.ShapeDtypeStruct((n, 512), f32),
    )
    out_specs = (
        pl.BlockSpec((None, 512, tm), slab), pl.BlockSpec((None, 512, tm), slab), pl.BlockSpec((None, 8, tm), slab),
        pl.BlockSpec((None, 512, tm), slab), pl.BlockSpec((None, 128, tm), slab), pl.BlockSpec((None, 512, tm), slab),
        pl.BlockSpec((1024, tm), tok), pl.BlockSpec((1024, tm), tok),
        pl.BlockSpec((tm, 128), row), pl.BlockSpec((tm, 128), row), pl.BlockSpec((tm, 64), row),
        pl.BlockSpec((tm, 512), row), pl.BlockSpec((tm, 512), row),
    )
    in_specs = [
        pl.BlockSpec((tm, d), row) if x_token_major else pl.BlockSpec((d, tm), tok),
        pl.BlockSpec((d, 1), const), pl.BlockSpec((N_IN_PAD, d), const),
        pl.BlockSpec((64, 1), const), pl.BlockSpec((64, 1), const), pl.BlockSpec((64, 1), const),
        pl.BlockSpec((32, tm), lambda i: (0, i % pos_blocks)),
        pl.BlockSpec((32, tm), lambda i: (0, i % pos_blocks)),
    ]
    return pl.pallas_call(
        functools.partial(_in_proj_kernel, x_token_major=x_token_major),
        grid=(nt,), in_specs=in_specs, out_specs=out_specs, out_shape=out_shape,
        compiler_params=_cparams(("parallel",)), name="in_proj",
    )(x, n1, w_inT, qn, kn, ikn, cosT, sinT)


def _extent_classes(t_q, s_total):
    if t_q % LANES:
        return [(0, 1, s_total)]
    nq = t_q // LANES
    cq = max(1, nq // N_EXTENT_CLASSES)
    assert nq % cq == 0
    return [(c * cq, cq, (c + 1) * cq * LANES) for c in range(nq // cq)]


def _query_positions(b, qblk, *, t_q, q_pos0, q_blk0):
    lane = lax.broadcasted_iota(jnp.int32, (1, LANES), 1)
    if t_q % LANES == 0:
        return q_pos0 + (q_blk0 + qblk) * LANES + lane, None
    return q_pos0 + lane % t_q, (lane // t_q) == b


def _store_queries(o_ref, rows, val, valid, b):
    if valid is None:
        o_ref[rows, :] = val
    else:
        @pl.when(b == 0)
        def _():
            o_ref[rows, :] = jnp.zeros(val.shape, o_ref.dtype)
        o_ref[rows, :] = jnp.where(valid, val, o_ref[rows, :])


def _pad_rows(x):
    return jnp.concatenate([x, jnp.zeros((LANES - x.shape[0], x.shape[1]), x.dtype)], axis=0)


def _keys_token_major(cache_ref, new_ref):
    if new_ref is None:
        return cache_ref[...]
    return jnp.concatenate([cache_ref[...], _pad_rows(new_ref[...])], axis=0)


def _values_feature_major(cache_t_ref, new_ref):
    if new_ref is None:
        return cache_t_ref[...]
    return jnp.concatenate([cache_t_ref[...], _pad_rows(new_ref[...]).T.astype(MXU_DTYPE)], axis=1)


def _float_order_key(x):
    bits = lax.bitcast_convert_type(x, jnp.int32)
    key = jnp.where(bits < 0, bits ^ jnp.int32(0x7FFFFFFF), bits)
    return jnp.where(x == 0.0, jnp.int32(0), key)


def _count(mask):
    return _col_reduce(jnp.where(mask, 1.0, 0.0), jnp.sum)


def _dsa_kernel(qa_ref, qi_ref, wi_ref, k_ref, vt_ref, ki_ref, *rest, t_q, q_pos0, q_blk0, s_valid, topk):
    b, qblk = pl.program_id(0), pl.program_id(1)
    (kn_ref, vn_ref, kin_ref), o_ref = (rest[:3] if len(rest) == 4 else (None, None, None)), rest[-1]
    k_all = _keys_token_major(k_ref, kn_ref)
    ki_all = _keys_token_major(ki_ref, kin_ref)
    vt_all = _values_feature_major(vt_ref, vn_ref)
    s_pad = k_all.shape[0]
    qpos, valid = _query_positions(b, qblk, t_q=t_q, q_pos0=q_pos0, q_blk0=q_blk0)
    kpos = lax.broadcasted_iota(jnp.int32, (s_pad, LANES), 0)
    adm = ((kpos // CHUNK) <= (qpos // CHUNK)) & (kpos < s_valid)

    qcat = jnp.concatenate([qi_ref[hh * 64:(hh + 1) * 64, :] for hh in range(IDX_HEADS)], axis=1)
    parts = []
    for r0 in range(0, s_pad, KEY_CHUNK):
        r1 = min(s_pad, r0 + KEY_CHUNK)
        z = _dot(ki_all[r0:r1, :].astype(MXU_DTYPE), qcat)
        acc = jnp.maximum(z[:, :LANES], 0.0) * wi_ref[0:1, :]
        for hh in range(1, IDX_HEADS):
            acc = acc + jnp.maximum(z[:, hh * LANES:(hh + 1) * LANES], 0.0) * wi_ref[hh:hh + 1, :]
        parts.append(acc)
    isc = jnp.concatenate(parts, axis=0) if len(parts) > 1 else parts[0]
    key = _float_order_key(jnp.where(adm, isc, NEG))

    kf = float(topk)
    int_min = jnp.int32(-2 ** 31)
    base = jnp.where(_count(key >= 0) >= kf, jnp.int32(0), int_min)

    def bit_step(i, base):
        cand = base + jnp.left_shift(jnp.int32(1), jnp.int32(30) - i)
        return jnp.where(_count(key >= cand) >= kf, cand, base)

    thr = lax.fori_loop(0, 31, bit_step, base)
    gt = key > thr
    tied = key == thr
    need = kf - _count(gt)
    row = lax.broadcasted_iota(jnp.int32, (LANES, LANES), 0)
    col = lax.broadcasted_iota(jnp.int32, (LANES, LANES), 1)
    tril = jnp.where(col <= row, 1.0, 0.0).astype(MXU_DTYPE)
    tied01 = jnp.where(tied, 1.0, 0.0).astype(MXU_DTYPE)
    carry = jnp.zeros((1, LANES), jnp.float32)
    ranks = []
    for c in range(s_pad // LANES):
        rank = _dot(tril, tied01[c * LANES:(c + 1) * LANES]) + carry
        ranks.append(rank)
        carry = rank[LANES - 1:LANES, :]
    rank = jnp.concatenate(ranks, axis=0) if len(ranks) > 1 else ranks[0]
    sel = gt | (tied & (rank <= need))
    bias = jnp.where(sel & adm, 0.0, NEG)

    kb16 = k_all.astype(MXU_DTYPE)
    zeros64 = jnp.zeros((64, LANES), MXU_DTYPE)
    for n in range(A_KV_HEADS):
        qs = []
        for g in range(A_GROUP):
            hh = n * A_GROUP + g
            qh = qa_ref[hh * 64:(hh + 1) * 64, :]
            qs.append(jnp.concatenate([qh, zeros64] if n == 0 else [zeros64, qh], axis=0))
        s = _dot(kb16, jnp.concatenate(qs, axis=1))
        ps, ls = [], []
        for g in range(A_GROUP):
            sg = s[:, g * LANES:(g + 1) * LANES] + bias
            p = jnp.exp(sg - _col_reduce(sg, jnp.max))
            ls.append(_col_reduce(p, jnp.sum))
            ps.append(p.astype(MXU_DTYPE))
        o = _dot(vt_all[n * 64:(n + 1) * 64, :], jnp.concatenate(ps, axis=1))
        for g in range(A_GROUP):
            hh = n * A_GROUP + g
            og = (o[:, g * LANES:(g + 1) * LANES] / ls[g]).astype(o_ref.dtype)
            _store_queries(o_ref, slice(hh * 64, (hh + 1) * 64), og, valid, b)


def _dsa(qaT, qiT, wiT, k, vT, ki, new, *, n_b, t_q, q_pos0, s_valid, topk):
    shared = t_q % LANES != 0
    s_cache = k.shape[1]
    outs = []
    for q_blk0, n_qblk, s_used in _extent_classes(t_q, s_cache):
        qmap = lambda b, q, q0=q_blk0: (0 if shared else b, 0, q0 + q)
        omap = lambda b, q: (0 if shared else b, 0, q)
        kern = functools.partial(_dsa_kernel, t_q=t_q, q_pos0=q_pos0, q_blk0=q_blk0, s_valid=s_valid, topk=topk)
        in_specs = [
            pl.BlockSpec((None, 512, LANES), qmap), pl.BlockSpec((None, 512, LANES), qmap),
            pl.BlockSpec((None, 8, LANES), qmap),
            pl.BlockSpec((None, s_used, 128), lambda b, q: (b, 0, 0)),
            pl.BlockSpec((None, 128, s_used), lambda b, q: (b, 0, 0)),
            pl.BlockSpec((None, s_used, 64), lambda b, q: (b, 0, 0)),
        ]
        args = [qaT, qiT, wiT, k, vT, ki]
        if new is not None:
            in_specs += [pl.BlockSpec((t_q, f), lambda b, q: (b, 0)) for f in (128, 128, 64)]
            args += list(new)
        outs.append(pl.pallas_call(
            kern, grid=(n_b, n_qblk), in_specs=in_specs,
            out_specs=pl.BlockSpec((None, 512, LANES), omap),
            out_shape=jax.ShapeDtypeStruct((qaT.shape[0], 512, n_qblk * LANES), MXU_DTYPE),
            compiler_params=_cparams(("arbitrary", "arbitrary")), name="dsa_attn",
        )(*args))
    return outs[0] if len(outs) == 1 else jnp.concatenate(outs, axis=2)


def _split3(x):
    top_bits = jnp.int32(-65536)
    trunc = lambda v: lax.bitcast_convert_type(lax.bitcast_convert_type(v, jnp.int32) & top_bits, jnp.float32)
    hi = trunc(x)
    r1 = x - hi
    mid = trunc(r1)
    return hi.astype(MXU_DTYPE), mid.astype(MXU_DTYPE), (r1 - mid).astype(MXU_DTYPE)


def _sb_weights(z_streams, qpos, first_qpos):
    n_blk = z_streams[0].shape[0] // LANES
    row = lax.broadcasted_iota(jnp.int32, (LANES, LANES), 0)
    col = lax.broadcasted_iota(jnp.int32, (LANES, LANES), 1)
    tri = jnp.where(col > row, 1.0, 0.0).astype(MXU_DTYPE)
    items = []
    for z_all in z_streams:
        for c in range(n_blk):
            z = z_all[c * LANES:(c + 1) * LANES]
            before = None if (c + 1) * LANES <= first_qpos else (c * LANES + row) < qpos
            ls = jnp.minimum(z, 0.0) - jnp.log(1.0 + jnp.exp(-jnp.abs(z)))
            lr = ls - z
            if before is not None:
                lr = jnp.where(before, lr, 0.0)
            items.append((ls, lr, before))
    afts = []
    for i in range(0, len(items), 2):
        group = items[i:i + 2]
        res = _dot(tri, jnp.concatenate([p for it in group for p in _split3(it[1])], axis=1))
        for j in range(len(group)):
            r = res[:, 3 * j * LANES:3 * (j + 1) * LANES]
            afts.append(r[:, :LANES] + r[:, LANES:2 * LANES] + r[:, 2 * LANES:])
    outs = []
    for s in range(len(z_streams)):
        rest_lr = jnp.zeros((1, LANES), jnp.float32)
        atts = [None] * n_blk
        for c in reversed(range(n_blk)):
            ls, lr, before = items[s * n_blk + c]
            att = jnp.exp(ls + afts[s * n_blk + c] + rest_lr)
            if before is not None:
                att = jnp.where(before, att, 0.0)
            atts[c] = att.astype(MXU_DTYPE)
            rest_lr = rest_lr + jnp.sum(lr, axis=0, keepdims=True)
        outs.append(jnp.concatenate(atts, axis=0) if n_blk > 1 else atts[0])
    return outs


def _sb_kernel(q_ref, k_ref, vt_ref, o_ref, *, q_blk0):
    qblk = pl.program_id(2)
    k16 = k_ref[...].astype(MXU_DTYPE)
    qpos, _ = _query_positions(0, qblk, t_q=LANES, q_pos0=0, q_blk0=q_blk0)
    zeros64 = jnp.zeros((64, LANES), MXU_DTYPE)
    zs = []
    for hh in range(2):
        qh = q_ref[hh * 64:(hh + 1) * 64, :]
        zs.append(_dot(k16, jnp.concatenate([qh, zeros64] if hh == 0 else [zeros64, qh], axis=0)))
    for hh, att in enumerate(_sb_weights(zs, qpos, q_blk0 * LANES)):
        o_ref[hh * 64:(hh + 1) * 64, :] = _dot(vt_ref[hh * 64:(hh + 1) * 64, :], att).astype(o_ref.dtype)


def _sb_step_kernel(q_ref, k_ref, vt_ref, kn_ref, vn_ref, o_ref, *, t_q, q_pos0):
    k16 = _keys_token_major(k_ref, kn_ref).astype(MXU_DTYPE)
    vt_all = _values_feature_major(vt_ref, vn_ref)
    lane = lax.broadcasted_iota(jnp.int32, (1, LANES), 1)
    (att,) = _sb_weights([_dot(k16, q_ref[...])], q_pos0 + lane % t_q, q_pos0)
    o_ref[...] = _dot(vt_all, att).astype(o_ref.dtype)


def _sb(qbT, kb, vbT, *, n_b, t_q):
    outs = []
    for q_blk0, n_qblk, s_used in _extent_classes(t_q, kb.shape[1]):
        outs.append(pl.pallas_call(
            functools.partial(_sb_kernel, q_blk0=q_blk0), grid=(B_HEADS // 2, n_b, n_qblk),
            in_specs=[
                pl.BlockSpec((None, 128, LANES), lambda hp, b, q, q0=q_blk0: (b, hp, q0 + q)),
                pl.BlockSpec((None, s_used, 128), lambda hp, b, q: (b, 0, hp)),
                pl.BlockSpec((None, 128, s_used), lambda hp, b, q: (b, hp, 0)),
            ],
            out_specs=pl.BlockSpec((None, 128, LANES), lambda hp, b, q: (b, hp, q)),
            out_shape=jax.ShapeDtypeStruct((n_b, 512, n_qblk * LANES), MXU_DTYPE),
            compiler_params=_cparams(("parallel", "parallel", "parallel")), name="sb_attn",
        )(qbT, kb, vbT))
    return outs[0] if len(outs) == 1 else jnp.concatenate(outs, axis=2)


def _sb_step(qbT, kb, vbT, new_kb, new_vb, *, n_b, t_q, q_pos0):
    assert B_HEADS * t_q == LANES
    p_len = kb.shape[1]
    eye = jnp.eye(B_HEADS, dtype=qbT.dtype)
    q4 = qbT.reshape(B_HEADS, HD, n_b, t_q).transpose(2, 0, 1, 3)
    q_heads = q4[:, :, :, None, :] * eye[None, :, None, :, None]
    q_heads = q_heads.reshape(n_b, B_HEADS * HD, LANES)
    o = pl.pallas_call(
        functools.partial(_sb_step_kernel, t_q=t_q, q_pos0=q_pos0), grid=(n_b,),
        in_specs=[
            pl.BlockSpec((None, 512, LANES), lambda b: (b, 0, 0)),
            pl.BlockSpec((None, p_len, 512), lambda b: (b, 0, 0)),
            pl.BlockSpec((None, 512, p_len), lambda b: (b, 0, 0)),
            pl.BlockSpec((t_q, 512), lambda b: (b, 0)), pl.BlockSpec((t_q, 512), lambda b: (b, 0)),
        ],
        out_specs=pl.BlockSpec((None, 512, LANES), lambda b: (b, 0, 0)),
        out_shape=jax.ShapeDtypeStruct((n_b, 512, LANES), jnp.float32),
        compiler_params=_cparams(("parallel",)), name="sb_step",
    )(q_heads, kb, vbT, new_kb, new_vb)
    o = jnp.diagonal(o.reshape(n_b, B_HEADS, HD, B_HEADS, t_q), axis1=1, axis2=3)
    return o.transpose(3, 1, 0, 2).reshape(1, B_HEADS * HD, n_b * t_q).astype(MXU_DTYPE)


def _top_values(s, n):
    rows = s.shape[0]
    rid = lax.broadcasted_iota(jnp.int32, s.shape, 0).astype(jnp.float32)
    vals = []
    for _ in range(n):
        m = jnp.max(s, axis=0, keepdims=True)
        first = jnp.min(jnp.where(s == m, rid, float(rows)), axis=0, keepdims=True)
        vals.append(m)
        s = jnp.where(rid == first, -jnp.inf, s)
    return vals


def _top_values_untied(s, n):
    n_inf = _col_reduce(jnp.where(s == -jnp.inf, 1.0, 0.0), jnp.sum)
    vals = []
    for _ in range(n):
        m = jnp.max(s, axis=0, keepdims=True)
        vals.append(m)
        s = jnp.where(s == m, -jnp.inf, s)
    extra = _col_reduce(jnp.where(s == -jnp.inf, 1.0, 0.0), jnp.sum) - n_inf - float(n)
    return vals, extra


def _expert_score_stats(s1, s2, top_values):
    r1, r2 = top_values(s1, P_TOPK + 1), top_values(s2, P_TOPK + 1)
    a1, a2 = (r1[0], r2[0]) if isinstance(r1, tuple) else (r1, r2)
    rc = top_values(_candidate_sums(a1, a2), P_TOPK + 1)
    top = rc[0] if isinstance(rc, tuple) else rc
    pad = [jnp.full_like(top[0], -jnp.inf)] * (3 * SUBLANES - (P_TOPK + 1))
    top = jnp.concatenate(top + pad, axis=0)
    if isinstance(rc, tuple):
        return top, jnp.maximum(jnp.maximum(r1[1], r2[1]), rc[1])
    return top


def _candidate_sums(a1, a2):
    tm = a1[0].shape[1]
    ninf = jnp.full((1, tm), -jnp.inf, jnp.float32)
    pad = [ninf] * (3 * SUBLANES - (P_TOPK + 1))
    a1m = jnp.concatenate(a1 + pad, axis=0)
    a2m = jnp.concatenate(a2 + pad, axis=0)
    r8 = lax.broadcasted_iota(jnp.int32, (SUBLANES, tm), 0)
    pieces = [a1[0] + a2m, a1[1] + a2m[0:SUBLANES]]
    for i in range(2, SUBLANES):
        pieces.append(jnp.where(r8 < (P_TOPK + 1) // (i + 1), a1[i] + a2m[0:SUBLANES], -jnp.inf))
    pieces.append(a1m[SUBLANES:] + a2[0])
    return jnp.concatenate(pieces, axis=0)


def _out_proj_kernel(x_ref, oa_ref, ob_ref, sga_ref, sgb_ref, wpa_ref, wpb_ref, wo_ref, n2_ref,
                     wpq_ref, k1_ref, k2_ref,
                     x1_ref, h2_ref, e1_ref, e2_ref, gthr_ref, q_scr, *, x_token_major):
    m = sga_ref[...] * _dot(wpa_ref[...], oa_ref[...]) + sgb_ref[...] * _dot(wpb_ref[...], ob_ref[...])
    x = x_ref[...].T if x_token_major else x_ref[...]
    x1 = x + _dot(wo_ref[...], m.astype(MXU_DTYPE))
    x1_ref[...] = x1
    ms = jnp.mean(x1 * x1, axis=0, keepdims=True)
    h2 = (x1 * lax.rsqrt(ms + EPS) * n2_ref[...]).astype(MXU_DTYPE)
    h2_ref[...] = h2
    q_scr[...] = _dot(wpq_ref[...], h2).astype(MXU_DTYPE)

    def scores(hh):
        r0 = pl.multiple_of(hh * 128, 128)
        s1 = _dot(k1_ref[hh], q_scr[pl.ds(r0, 64), :])
        s2 = _dot(k2_ref[hh], q_scr[pl.ds(pl.multiple_of(r0 + 64, 64), 64), :])
        return (s1, s2) + _expert_score_stats(s1, s2, _top_values_untied)

    def finish(hh, s1, s2, top, extra):
        top = lax.cond(jnp.max(jnp.abs(extra)) > 0.0,
                       lambda: _expert_score_stats(s1, s2, _top_values), lambda: top)
        mx = top[0:1]
        z = jnp.zeros_like(mx)
        for i in range(P_TOPK):
            z = z + jnp.exp(top[i:i + 1] - mx)
        rz = 0.5 / z
        e1_ref[hh] = jnp.exp(s1 - jnp.max(s1, axis=0, keepdims=True)) * rz
        e2_ref[hh] = jnp.exp(s2 - jnp.max(s2, axis=0, keepdims=True))
        gthr_ref[pl.ds(hh, 1), :] = jnp.exp(0.5 * (top[P_TOPK - 1:P_TOPK] + top[P_TOPK:P_TOPK + 1]) - mx) * rz

    def head_group(i, carry):
        heads = [HEADS_PER_TRIP * i + j for j in range(HEADS_PER_TRIP)]
        stats = [scores(hh) for hh in heads]
        for hh, st in zip(heads, stats):
            finish(hh, *st)
        return carry

    lax.fori_loop(0, P_HEADS // HEADS_PER_TRIP, head_group, 0)


def _out_proj(x, oaT, obT, sgaT, sgbT, w_paT, w_pbT, w_oT, n2, w_pqT, k1, k2, tm, x_token_major):
    n, d = x.shape if x_token_major else x.shape[::-1]
    nt = n // tm
    tpb = oaT.shape[2] // tm
    f32, bf = jnp.float32, MXU_DTYPE
    tok = lambda i: (0, i)
    tok3 = lambda i: (0, 0, i)
    slab = lambda i: (i // tpb, 0, i % tpb)
    c2 = lambda i: (0, 0)
    c3 = lambda i: (0, 0, 0)
    out_shape = (
        jax.ShapeDtypeStruct((d, n), f32), jax.ShapeDtypeStruct((d, n), bf),
        jax.ShapeDtypeStruct((P_HEADS, N_KEYS, n), f32), jax.ShapeDtypeStruct((P_HEADS, N_KEYS, n), f32),
        jax.ShapeDtypeStruct((P_HEADS, n), f32),
    )
    out_specs = (
        pl.BlockSpec((d, tm), tok), pl.BlockSpec((d, tm), tok),
        pl.BlockSpec((P_HEADS, N_KEYS, tm), tok3), pl.BlockSpec((P_HEADS, N_KEYS, tm), tok3),
        pl.BlockSpec((P_HEADS, tm), tok),
    )
    in_specs = [
        pl.BlockSpec((tm, d), lambda i: (i, 0)) if x_token_major else pl.BlockSpec((d, tm), tok),
        pl.BlockSpec((None, 512, tm), slab), pl.BlockSpec((None, 512, tm), slab),
        pl.BlockSpec((d, tm), tok), pl.BlockSpec((d, tm), tok),
        pl.BlockSpec((d, 512), c2), pl.BlockSpec((d, 512), c2), pl.BlockSpec((d, d), c2),
        pl.BlockSpec((d, 1), c2), pl.BlockSpec((P_HEADS * 128, d), c2),
        pl.BlockSpec((P_HEADS, N_KEYS, P_HALF), c3), pl.BlockSpec((P_HEADS, N_KEYS, P_HALF), c3),
    ]
    return pl.pallas_call(
        functools.partial(_out_proj_kernel, x_token_major=x_token_major),
        grid=(nt,), in_specs=in_specs, out_specs=out_specs, out_shape=out_shape,
        scratch_shapes=[pltpu.VMEM((P_HEADS * 128, tm), bf)],
        compiler_params=_cparams(("parallel",)), name="out_proj",
    )(x, oaT, obT, sgaT, sgbT, w_paT, w_pbT, w_oT, n2, w_pqT, k1, k2)


def _twice_gelu_exact(x):
    return x * (1.0 + lax.erf(x * np.float32(math.sqrt(0.5))))


def _peer_kernel(h2_ref, u_ref, vt_ref, e1_ref, e2_ref, gthr_ref, x1_ref, o_ref, acc_ref, coef_ref, *,
                 out_token_major):
    ec = pl.program_id(1)
    tm = h2_ref.shape[1]

    @pl.when(ec == 0)
    def _():
        acc_ref[...] = jnp.zeros_like(acc_ref)

    a_all = _dot(u_ref[...], h2_ref[...])
    n_groups = TE_PEER // (SUBLANES * N_KEYS)
    for grp in range(n_groups):
        i1_group = pl.ds(pl.multiple_of((ec * n_groups + grp) * SUBLANES, SUBLANES), SUBLANES)
        for c in range(tm // LANES):
            cs = slice(c * LANES, (c + 1) * LANES)
            e1g = [e1_ref[hh, i1_group, cs] for hh in range(P_HEADS)]
            gth = [gthr_ref[hh:hh + 1, cs] for hh in range(P_HEADS)]
            for il in range(SUBLANES):
                gate = jnp.zeros((N_KEYS, LANES), jnp.float32)
                for hh in range(P_HEADS):
                    g = e1g[hh][il:il + 1, :] * e2_ref[hh, :, cs]
                    gate = gate + jnp.where(g >= gth[hh], g, 0.0)
                rows = slice((grp * SUBLANES + il) * N_KEYS, (grp * SUBLANES + il + 1) * N_KEYS)
                coef_ref[rows, cs] = (gate * _twice_gelu_exact(a_all[rows, cs])).astype(coef_ref.dtype)
    acc_ref[...] += _dot(vt_ref[...], coef_ref[...])

    @pl.when(ec == pl.num_programs(1) - 1)
    def _():
        x2 = x1_ref[...] + acc_ref[...]
        o_ref[...] = x2.T if out_token_major else x2


def _peer(h2T, u, vT, e1T, e2T, gthr, x1T, tm, out_token_major):
    d, n = x1T.shape
    n_exp = u.shape[0]
    tok = lambda i, e: (0, i)
    tok3 = lambda i, e: (0, 0, i)
    in_specs = [
        pl.BlockSpec((d, tm), tok),
        pl.BlockSpec((TE_PEER, d), lambda i, e: (e, 0)),
        pl.BlockSpec((d, TE_PEER), lambda i, e: (0, e)),
        pl.BlockSpec((P_HEADS, N_KEYS, tm), tok3), pl.BlockSpec((P_HEADS, N_KEYS, tm), tok3),
        pl.BlockSpec((P_HEADS, tm), tok),
        pl.BlockSpec((d, tm), tok),
    ]
    return pl.pallas_call(
        functools.partial(_peer_kernel, out_token_major=out_token_major), grid=(n // tm, n_exp // TE_PEER),
        in_specs=in_specs,
        out_specs=pl.BlockSpec((tm, d), lambda i, e: (i, 0)) if out_token_major else pl.BlockSpec((d, tm), tok),
        out_shape=jax.ShapeDtypeStruct((n, d) if out_token_major else (d, n), jnp.float32),
        scratch_shapes=[pltpu.VMEM((d, tm), jnp.float32), pltpu.VMEM((TE_PEER, tm), MXU_DTYPE)],
        compiler_params=_cparams(("parallel", "arbitrary")), name="peer_dense",
    )(h2T, u, vT, e1T, e2T, gthr, x1T)


def _pack_w_in(w_in):
    d = w_in.shape[0]
    cols = [w_in[:, 0:1280], w_in[:, 1280:1352], jnp.zeros((d, 56), w_in.dtype), w_in[:, 1352:]]
    return jnp.concatenate(cols, axis=1).T.astype(MXU_DTYPE)


def _rope_tables(pos):
    half = HD // 2
    inv = ROPE_THETA ** (-jnp.arange(half, dtype=jnp.float32) / half)
    ang = pos.astype(jnp.float32)[None, :] * inv[:, None]
    return jnp.cos(ang), jnp.sin(ang)


def _token_tile(n, pref):
    return pref if n % pref == 0 else LANES


def _layer(x, lw, rope, n_b, t_q, q_pos0, past, topk, x_token_major, out_token_major):
    (n1, w_inT, qn, kn, ikn, w_paT, w_pbT, w_oT, n2, w_pqT, k1, k2, u, vT) = lw
    n = n_b * t_q
    tm = _token_tile(n, TM_PROJ)
    n_bo = n_b if past is None else 1
    (qaT, qiT, wiT, qbT, vaT, vbT, sgaT, sgbT, ka, va, ki, kb, vb) = _in_proj(
        x, n1, w_inT, qn, kn, ikn, rope[0], rope[1], tm, n_bo, x_token_major)
    new = (ka, va, ki, kb, vb)
    if past is None:
        s_valid = t_q
        k_a = ka.reshape(n_b, t_q, 128)
        k_i = ki.reshape(n_b, t_q, 64)
        k_b = kb.reshape(n_b, t_q, 512)
        v_aT, v_bT = vaT, vbT
        new_a = None
    else:
        k_a, v_aT, k_i, k_b, v_bT = past
        s_valid = k_a.shape[1] + t_q
        new_a = (ka, va, ki)
    oaT = _dsa(qaT, qiT, wiT, k_a, v_aT, k_i, new_a, n_b=n_b, t_q=t_q, q_pos0=q_pos0, s_valid=s_valid, topk=topk)
    if past is None:
        obT = _sb(qbT, k_b, v_bT, n_b=n_b, t_q=t_q)
    else:
        obT = _sb_step(qbT, k_b, v_bT, kb, vb, n_b=n_b, t_q=t_q, q_pos0=q_pos0)
    x1T, h2T, e1T, e2T, gthr = _out_proj(
        x, oaT, obT, sgaT, sgbT, w_paT, w_pbT, w_oT, n2, w_pqT, k1, k2, tm, x_token_major)
    x2 = _peer(h2T, u, vT, e1T, e2T, gthr, x1T, _token_tile(n, TM_PEER), out_token_major)
    return x2, new


def kernel(x_prompt, x_sample, cache_a_k, cache_a_v, cache_idx_k, cache_b_k, cache_b_v, norm1, w_in, q_norm_a, k_norm_a, idx_k_norm, w_pa, w_pb, w_o, norm2, peer_wq, peer_k1, peer_k2, peer_u, peer_v):
    n_bp, t_p, d = x_prompt.shape
    n_bs, t_s, _ = x_sample.shape
    depth = w_in.shape[0]
    past_len = cache_a_k.shape[2]
    assert t_p % LANES == 0 and n_bs * t_s == LANES and past_len % LANES == 0 and t_s % SUBLANES == 0
    assert peer_u.shape[1] % TE_PEER == 0
    topk_p = min(TOPK_MAX, t_p // 4)
    topk_s = min(TOPK_MAX, (past_len + t_s) // 4)
    bf = MXU_DTYPE
    col = lambda g: g.reshape(-1, 1)

    rope_p = _rope_tables(jnp.arange(t_p, dtype=jnp.int32))
    rope_s = _rope_tables(jnp.tile(past_len + jnp.arange(t_s, dtype=jnp.int32), n_bs))

    xp = x_prompt.reshape(n_bp * t_p, d)
    xs = x_sample.reshape(n_bs * t_s, d)
    st_p, st_s = [], []
    for l in range(depth):
        first, last = l == 0, l == depth - 1
        lw = (col(norm1[l]), _pack_w_in(w_in[l]), col(q_norm_a[l]), col(k_norm_a[l]), col(idx_k_norm[l]),
              w_pa[l].T.astype(bf), w_pb[l].T.astype(bf), w_o[l].T.astype(bf), col(norm2[l]),
              peer_wq[l].T.astype(bf), peer_k1[l].astype(bf), peer_k2[l].astype(bf),
              peer_u[l].astype(bf), peer_v[l].T.astype(bf))
        past = (cache_a_k[l].reshape(n_bs, past_len, -1),
                cache_a_v[l].reshape(n_bs, past_len, -1).transpose(0, 2, 1).astype(bf),
                cache_idx_k[l],
                cache_b_k[l].reshape(n_bs, past_len, -1),
                cache_b_v[l].reshape(n_bs, past_len, -1).transpose(0, 2, 1).astype(bf))
        xp, sp = _layer(xp, lw, rope_p, n_bp, t_p, 0, None, topk_p, first, last)
        xs, ss = _layer(xs, lw, rope_s, n_bs, t_s, past_len, past, topk_s, first, last)
        st_p.append(sp)
        st_s.append(ss)

    def stack(sts, j, n_b, t, shape):
        return jnp.stack([s[j] for s in sts]).reshape((depth, n_b, t) + shape)

    outs = [xp.reshape(n_bp, t_p, d), xs.reshape(n_bs, t_s, d)]
    shapes = ((A_KV_HEADS, HD), (A_KV_HEADS, HD), (IDX_DIM,), (B_HEADS, HD), (B_HEADS, HD))
    for sts, n_b, t in ((st_p, n_bp, t_p), (st_s, n_bs, t_s)):
        for j, shp in enumerate(shapes):
            outs.append(stack(sts, j, n_b, t, shp))
    return tuple(outs)
```

```python
import functools
import math

import jax
import jax.numpy as jnp
import numpy as np
from jax import lax
from jax.experimental import pallas as pl
from jax.experimental.pallas import tpu as pltpu

CHUNK = 64
EPS = 1e-6
ROPE_THETA = 10000.0
NEG = -1e30
A_HEADS, A_KV_HEADS, HD = 8, 2, 64
A_GROUP = A_HEADS // A_KV_HEADS
IDX_HEADS, IDX_DIM = 8, 64
TOPK_MAX = 256
B_HEADS = 8
P_HEADS, N_KEYS, P_HALF, P_TOPK = 8, 128, 64, 16

LANES = 128
SUBLANES = 8
VMEM_LIMIT = 56 * 1024 * 1024
TM_IN_PROJ = 256
TM_OUT_PROJ = 512
TM_PEER = 512
TE_PEER = 2 * SUBLANES * N_KEYS
KEY_CHUNK = 512
HEADS_PER_TRIP = 4
N_EXTENT_CLASSES = 16

MXU_DTYPE = jnp.bfloat16

SEG_QA, SEG_KA, SEG_VA, SEG_QI, SEG_KIWI = 0, 512, 640, 768, 1280
SEG_QB, SEG_KB, SEG_VB, SEG_GA, SEG_GB, N_IN_PAD = 1408, 1920, 2432, 2944, 3968, 4992


def _cparams(sem):
    return pltpu.CompilerParams(dimension_semantics=sem, vmem_limit_bytes=VMEM_LIMIT)


def _dot(a, b):
    return jnp.dot(a, b, preferred_element_type=jnp.float32)


def _col_reduce(x, op):
    rows = x.shape[0]
    if rows % 64 == 0 and rows > 64:
        x = op(x.reshape(rows // 64, 64, x.shape[1]), axis=0)
    return op(x, axis=0, keepdims=True)


def _rms_rope_head(blk, gain, cos, sin, scale):
    if gain is not None:
        ms = jnp.mean(blk * blk, axis=0, keepdims=True)
        blk = blk * lax.rsqrt(ms + EPS) * gain
    x1, x2 = blk[:32], blk[32:]
    o1 = x1 * cos - x2 * sin
    o2 = x2 * cos + x1 * sin
    if scale != 1.0:
        o1, o2 = o1 * scale, o2 * scale
    return o1, o2


def _in_proj_kernel(x_ref, n1_ref, w_ref, qn_ref, kn_ref, ikn_ref, cos_ref, sin_ref,
                    qa_ref, qi_ref, wi_ref, qb_ref, vat_ref, vbt_ref, sga_ref, sgb_ref,
                    ka_ref, va_ref, ki_ref, kb_ref, vb_ref, *, x_token_major):
    x = x_ref[...].T if x_token_major else x_ref[...]
    ms = jnp.mean(x * x, axis=0, keepdims=True)
    h = (x * lax.rsqrt(ms + EPS) * n1_ref[...]).astype(MXU_DTYPE)
    cos, sin = cos_ref[...], sin_ref[...]
    tm = x.shape[1]

    def seg(start, size):
        return _dot(w_ref[start:start + size, :], h)

    y = seg(SEG_QA, 512)
    for hh in range(A_HEADS):
        o1, o2 = _rms_rope_head(y[hh * 64:(hh + 1) * 64], qn_ref[...], cos, sin, HD ** -0.5)
        qa_ref[hh * 64:hh * 64 + 32, :] = o1.astype(qa_ref.dtype)
        qa_ref[hh * 64 + 32:(hh + 1) * 64, :] = o2.astype(qa_ref.dtype)
    y = seg(SEG_KA, 256)
    parts = []
    for hh in range(A_KV_HEADS):
        o1, o2 = _rms_rope_head(y[hh * 64:(hh + 1) * 64], kn_ref[...], cos, sin, 1.0)
        parts += [o1, o2]
    ka_ref[...] = jnp.concatenate(parts, axis=0).T
    va_t = y[128:256]
    va_ref[...] = va_t.T
    vat_ref[...] = va_t.astype(vat_ref.dtype)
    y = seg(SEG_QI, 512)
    for hh in range(IDX_HEADS):
        o1, o2 = _rms_rope_head(y[hh * 64:(hh + 1) * 64], None, cos, sin, IDX_DIM ** -0.5)
        qi_ref[hh * 64:hh * 64 + 32, :] = o1.astype(qi_ref.dtype)
        qi_ref[hh * 64 + 32:(hh + 1) * 64, :] = o2.astype(qi_ref.dtype)
    y = seg(SEG_KIWI, 128)
    o1, o2 = _rms_rope_head(y[0:64], ikn_ref[...], cos, sin, 1.0)
    ki_t = jnp.concatenate([o1, o2, jnp.zeros((64, tm), jnp.float32)], axis=0)
    ki_ref[...] = ki_t.T[:, :64]
    wi_ref[...] = y[64:72] * (IDX_HEADS ** -0.5)
    qb_ref[...] = (seg(SEG_QB, 512) * (HD ** -0.5)).astype(qb_ref.dtype)
    kb_ref[...] = seg(SEG_KB, 512).T
    y = seg(SEG_VB, 512)
    vb_ref[...] = y.T
    vbt_ref[...] = y.astype(vbt_ref.dtype)
    sga_ref[...] = jax.nn.sigmoid(seg(SEG_GA, 1024))
    sgb_ref[...] = jax.nn.sigmoid(seg(SEG_GB, 1024))


def _in_proj(x, n1, w_inT, qn, kn, ikn, cosT, sinT, tm, n_bo, x_token_major):
    n, d = x.shape if x_token_major else x.shape[::-1]
    nt = n // tm
    t_o = n // n_bo
    tpb = t_o // tm
    pos_blocks = cosT.shape[1] // tm
    f32, bf = jnp.float32, MXU_DTYPE
    tok = lambda i: (0, i)
    row = lambda i: (i, 0)
    const = lambda i: (0, 0)
    slab = lambda i: (i // tpb, 0, i % tpb)
    fm = lambda f, dt: jax.ShapeDtypeStruct((n_bo, f, t_o), dt)
    out_shape = (
        fm(512, bf), fm(512, bf), fm(8, f32), fm(512, bf), fm(128, bf), fm(512, bf),
        jax.ShapeDtypeStruct((1024, n), f32),
        jax.ShapeDtypeStruct((1024, n), f32),
        jax.ShapeDtypeStruct((n, 128), f32),
        jax.ShapeDtypeStruct((n, 128), f32),
        jax.ShapeDtypeStruct((n, 64), f32),
        jax.ShapeDtypeStruct((n, 512), f32),
        jax.ShapeDtypeStruct((n, 512), f32),
    )
    out_specs = (
        pl.BlockSpec((None, 512, tm), slab), pl.BlockSpec((None, 512, tm), slab), pl.BlockSpec((None, 8, tm), slab),
        pl.BlockSpec((None, 512, tm), slab), pl.BlockSpec((None, 128, tm), slab), pl.BlockSpec((None, 512, tm), slab),
        pl.BlockSpec((1024, tm), tok), pl.BlockSpec((1024, tm), tok),
        pl.BlockSpec((tm, 128), row), pl.BlockSpec((tm, 128), row), pl.BlockSpec((tm, 64), row),
        pl.BlockSpec((tm, 512), row), pl.BlockSpec((tm, 512), row),
    )
    in_specs = [
        pl.BlockSpec((tm, d), row) if x_token_major else pl.BlockSpec((d, tm), tok),
        pl.BlockSpec((d, 1), const), pl.BlockSpec((N_IN_PAD, d), const),
        pl.BlockSpec((64, 1), const), pl.BlockSpec((64, 1), const), pl.BlockSpec((64, 1), const),
        pl.BlockSpec((32, tm), lambda i: (0, i % pos_blocks)),
        pl.BlockSpec((32, tm), lambda i: (0, i % pos_blocks)),
    ]
    return pl.pallas_call(
        functools.partial(_in_proj_kernel, x_token_major=x_token_major),
        grid=(nt,), in_specs=in_specs, out_specs=out_specs, out_shape=out_shape,
        compiler_params=_cparams(("parallel",)), name="in_proj",
    )(x, n1, w_inT, qn, kn, ikn, cosT, sinT)


def _extent_classes(t_q, s_total):
    if t_q % LANES:
        return [(0, 1, s_total)]
    nq = t_q // LANES
    cq = max(1, nq // N_EXTENT_CLASSES)
    assert nq % cq == 0
    return [(c * cq, cq, (c + 1) * cq * LANES) for c in range(nq // cq)]


def _query_positions(b, qblk, *, t_q, q_pos0, q_blk0):
    lane = lax.broadcasted_iota(jnp.int32, (1, LANES), 1)
    if t_q % LANES == 0:
        return q_pos0 + (q_blk0 + qblk) * LANES + lane, None
    return q_pos0 + lane % t_q, (lane // t_q) == b


def _store_queries(o_ref, rows, val, valid, b):
    if valid is None:
        o_ref[rows, :] = val
    else:
        @pl.when(b == 0)
        def _():
            o_ref[rows, :] = jnp.zeros(val.shape, o_ref.dtype)
        o_ref[rows, :] = jnp.where(valid, val, o_ref[rows, :])


def _pad_rows(x):
    return jnp.concatenate([x, jnp.zeros((LANES - x.shape[0], x.shape[1]), x.dtype)], axis=0)


def _keys_token_major(cache_ref, new_ref):
    if new_ref is None:
        return cache_ref[...]
    return jnp.concatenate([cache_ref[...], _pad_rows(new_ref[...])], axis=0)


def _values_feature_major(cache_t_ref, new_ref):
    if new_ref is None:
        return cache_t_ref[...]
    return jnp.concatenate([cache_t_ref[...], _pad_rows(new_ref[...]).T.astype(MXU_DTYPE)], axis=1)


def _float_order_key(x):
    bits = lax.bitcast_convert_type(x, jnp.int32)
    key = jnp.where(bits < 0, bits ^ jnp.int32(0x7FFFFFFF), bits)
    return jnp.where(x == 0.0, jnp.int32(0), key)


def _count(mask):
    return _col_reduce(jnp.where(mask, 1.0, 0.0), jnp.sum)


def _dsa_kernel(qa_ref, qi_ref, wi_ref, k_ref, vt_ref, ki_ref, *rest, t_q, q_pos0, q_blk0, s_valid, topk):
    b, qblk = pl.program_id(0), pl.program_id(1)
    (kn_ref, vn_ref, kin_ref), o_ref = (rest[:3] if len(rest) == 4 else (None, None, None)), rest[-1]
    k_all = _keys_token_major(k_ref, kn_ref)
    ki_all = _keys_token_major(ki_ref, kin_ref)
    vt_all = _values_feature_major(vt_ref, vn_ref)
    s_pad = k_all.shape[0]
    qpos, valid = _query_positions(b, qblk, t_q=t_q, q_pos0=q_pos0, q_blk0=q_blk0)
    kpos = lax.broadcasted_iota(jnp.int32, (s_pad, LANES), 0)
    adm = ((kpos // CHUNK) <= (qpos // CHUNK)) & (kpos < s_valid)

    qcat = jnp.concatenate([qi_ref[hh * 64:(hh + 1) * 64, :] for hh in range(IDX_HEADS)], axis=1)
    parts = []
    for r0 in range(0, s_pad, KEY_CHUNK):
        r1 = min(s_pad, r0 + KEY_CHUNK)
        z = _dot(ki_all[r0:r1, :].astype(MXU_DTYPE), qcat)
        acc = jnp.maximum(z[:, :LANES], 0.0) * wi_ref[0:1, :]
        for hh in range(1, IDX_HEADS):
            acc = acc + jnp.maximum(z[:, hh * LANES:(hh + 1) * LANES], 0.0) * wi_ref[hh:hh + 1, :]
        parts.append(acc)
    isc = jnp.concatenate(parts, axis=0) if len(parts) > 1 else parts[0]
    key = _float_order_key(jnp.where(adm, isc, NEG))

    kf = float(topk)
    int_min = jnp.int32(-2 ** 31)
    base = jnp.where(_count(key >= 0) >= kf, jnp.int32(0), int_min)

    def bit_step(i, base):
        cand = base + jnp.left_shift(jnp.int32(1), jnp.int32(30) - i)
        return jnp.where(_count(key >= cand) >= kf, cand, base)

    thr = lax.fori_loop(0, 31, bit_step, base)
    gt = key > thr
    tied = key == thr
    need = kf - _count(gt)
    row = lax.broadcasted_iota(jnp.int32, (LANES, LANES), 0)
    col = lax.broadcasted_iota(jnp.int32, (LANES, LANES), 1)
    tril = jnp.where(col <= row, 1.0, 0.0).astype(MXU_DTYPE)
    tied01 = jnp.where(tied, 1.0, 0.0).astype(MXU_DTYPE)
    carry = jnp.zeros((1, LANES), jnp.float32)
    ranks = []
    for c in range(s_pad // LANES):
        rank = _dot(tril, tied01[c * LANES:(c + 1) * LANES]) + carry
        ranks.append(rank)
        carry = rank[LANES - 1:LANES, :]
    rank = jnp.concatenate(ranks, axis=0) if len(ranks) > 1 else ranks[0]
    sel = gt | (tied & (rank <= need))
    bias = jnp.where(sel & adm, 0.0, NEG)

    kb16 = k_all.astype(MXU_DTYPE)
    zeros64 = jnp.zeros((64, LANES), MXU_DTYPE)
    for n in range(A_KV_HEADS):
        qs = []
        for g in range(A_GROUP):
            hh = n * A_GROUP + g
            qh = qa_ref[hh * 64:(hh + 1) * 64, :]
            qs.append(jnp.concatenate([qh, zeros64] if n == 0 else [zeros64, qh], axis=0))
        s = _dot(kb16, jnp.concatenate(qs, axis=1))
        ps, ls = [], []
        for g in range(A_GROUP):
            sg = s[:, g * LANES:(g + 1) * LANES] + bias
            p = jnp.exp(sg - _col_reduce(sg, jnp.max))
            ls.append(_col_reduce(p, jnp.sum))
            ps.append(p.astype(MXU_DTYPE))
        o = _dot(vt_all[n * 64:(n + 1) * 64, :], jnp.concatenate(ps, axis=1))
        for g in range(A_GROUP):
            hh = n * A_GROUP + g
            og = (o[:, g * LANES:(g + 1) * LANES] / ls[g]).astype(o_ref.dtype)
            _store_queries(o_ref, slice(hh * 64, (hh + 1) * 64), og, valid, b)


def _dsa(qaT, qiT, wiT, k, vT, ki, new, *, n_b, t_q, q_pos0, s_valid, topk):
    shared = t_q % LANES != 0
    s_cache = k.shape[1]
    outs = []
    for q_blk0, n_qblk, s_used in _extent_classes(t_q, s_cache):
        qmap = lambda b, q, q0=q_blk0: (0 if shared else b, 0, q0 + q)
        omap = lambda b, q: (0 if shared else b, 0, q)
        kern = functools.partial(_dsa_kernel, t_q=t_q, q_pos0=q_pos0, q_blk0=q_blk0, s_valid=s_valid, topk=topk)
        in_specs = [
            pl.BlockSpec((None, 512, LANES), qmap), pl.BlockSpec((None, 512, LANES), qmap),
            pl.BlockSpec((None, 8, LANES), qmap),
            pl.BlockSpec((None, s_used, 128), lambda b, q: (b, 0, 0)),
            pl.BlockSpec((None, 128, s_used), lambda b, q: (b, 0, 0)),
            pl.BlockSpec((None, s_used, 64), lambda b, q: (b, 0, 0)),
        ]
        args = [qaT, qiT, wiT, k, vT, ki]
        if new is not None:
            in_specs += [pl.BlockSpec((t_q, f), lambda b, q: (b, 0)) for f in (128, 128, 64)]
            args += list(new)
        outs.append(pl.pallas_call(
            kern, grid=(n_b, n_qblk), in_specs=in_specs,
            out_specs=pl.BlockSpec((None, 512, LANES), omap),
            out_shape=jax.ShapeDtypeStruct((qaT.shape[0], 512, n_qblk * LANES), MXU_DTYPE),
            compiler_params=_cparams(("arbitrary", "arbitrary")), name="dsa_attn",
        )(*args))
    return outs[0] if len(outs) == 1 else jnp.concatenate(outs, axis=2)


def _split3(x):
    top_bits = jnp.int32(-65536)
    trunc = lambda v: lax.bitcast_convert_type(lax.bitcast_convert_type(v, jnp.int32) & top_bits, jnp.float32)
    hi = trunc(x)
    r1 = x - hi
    mid = trunc(r1)
    return hi.astype(MXU_DTYPE), mid.astype(MXU_DTYPE), (r1 - mid).astype(MXU_DTYPE)


def _sb_weights(z_streams, qpos, first_qpos):
    n_blk = z_streams[0].shape[0] // LANES
    row = lax.broadcasted_iota(jnp.int32, (LANES, LANES), 0)
    col = lax.broadcasted_iota(jnp.int32, (LANES, LANES), 1)
    tri = jnp.where(col > row, 1.0, 0.0).astype(MXU_DTYPE)
    items = []
    for z_all in z_streams:
        for c in range(n_blk):
            z = z_all[c * LANES:(c + 1) * LANES]
            before = None if (c + 1) * LANES <= first_qpos else (c * LANES + row) < qpos
            ls = jnp.minimum(z, 0.0) - jnp.log(1.0 + jnp.exp(-jnp.abs(z)))
            lr = ls - z
            if before is not None:
                lr = jnp.where(before, lr, 0.0)
            items.append((ls, lr, before))
    afts = []
    for i in range(0, len(items), 2):
        group = items[i:i + 2]
        res = _dot(tri, jnp.concatenate([p for it in group for p in _split3(it[1])], axis=1))
        for j in range(len(group)):
            r = res[:, 3 * j * LANES:3 * (j + 1) * LANES]
            afts.append(r[:, :LANES] + r[:, LANES:2 * LANES] + r[:, 2 * LANES:])
    outs = []
    for s in range(len(z_streams)):
        rest_lr = jnp.zeros((1, LANES), jnp.float32)
        atts = [None] * n_blk
        for c in reversed(range(n_blk)):
            ls, lr, before = items[s * n_blk + c]
            att = jnp.exp(ls + afts[s * n_blk + c] + rest_lr)
            if before is not None:
                att = jnp.where(before, att, 0.0)
            atts[c] = att.astype(MXU_DTYPE)
            rest_lr = rest_lr + jnp.sum(lr, axis=0, keepdims=True)
        outs.append(jnp.concatenate(atts, axis=0) if n_blk > 1 else atts[0])
    return outs


def _sb_kernel(q_ref, k_ref, vt_ref, o_ref, *, q_blk0):
    qblk = pl.program_id(2)
    k16 = k_ref[...].astype(MXU_DTYPE)
    qpos, _ = _query_positions(0, qblk, t_q=LANES, q_pos0=0, q_blk0=q_blk0)
    zeros64 = jnp.zeros((64, LANES), MXU_DTYPE)
    zs = []
    for hh in range(2):
        qh = q_ref[hh * 64:(hh + 1) * 64, :]
        zs.append(_dot(k16, jnp.concatenate([qh, zeros64] if hh == 0 else [zeros64, qh], axis=0)))
    for hh, att in enumerate(_sb_weights(zs, qpos, q_blk0 * LANES)):
        o_ref[hh * 64:(hh + 1) * 64, :] = _dot(vt_ref[hh * 64:(hh + 1) * 64, :], att).astype(o_ref.dtype)


def _sb_step_kernel(q_ref, k_ref, vt_ref, kn_ref, vn_ref, o_ref, *, t_q, q_pos0):
    k16 = _keys_token_major(k_ref, kn_ref).astype(MXU_DTYPE)
    vt_all = _values_feature_major(vt_ref, vn_ref)
    lane = lax.broadcasted_iota(jnp.int32, (1, LANES), 1)
    (att,) = _sb_weights([_dot(k16, q_ref[...])], q_pos0 + lane % t_q, q_pos0)
    o_ref[...] = _dot(vt_all, att).astype(o_ref.dtype)


def _sb(qbT, kb, vbT, *, n_b, t_q):
    outs = []
    for q_blk0, n_qblk, s_used in _extent_classes(t_q, kb.shape[1]):
        outs.append(pl.pallas_call(
            functools.partial(_sb_kernel, q_blk0=q_blk0), grid=(B_HEADS // 2, n_b, n_qblk),
            in_specs=[
                pl.BlockSpec((None, 128, LANES), lambda hp, b, q, q0=q_blk0: (b, hp, q0 + q)),
                pl.BlockSpec((None, s_used, 128), lambda hp, b, q: (b, 0, hp)),
                pl.BlockSpec((None, 128, s_used), lambda hp, b, q: (b, hp, 0)),
            ],
            out_specs=pl.BlockSpec((None, 128, LANES), lambda hp, b, q: (b, hp, q)),
            out_shape=jax.ShapeDtypeStruct((n_b, 512, n_qblk * LANES), MXU_DTYPE),
            compiler_params=_cparams(("parallel", "parallel", "parallel")), name="sb_attn",
        )(qbT, kb, vbT))
    return outs[0] if len(outs) == 1 else jnp.concatenate(outs, axis=2)


def _sb_step(qbT, kb, vbT, new_kb, new_vb, *, n_b, t_q, q_pos0):
    assert B_HEADS * t_q == LANES
    p_len = kb.shape[1]
    eye = jnp.eye(B_HEADS, dtype=qbT.dtype)
    q4 = qbT.reshape(B_HEADS, HD, n_b, t_q).transpose(2, 0, 1, 3)
    q_heads = q4[:, :, :, None, :] * eye[None, :, None, :, None]
    q_heads = q_heads.reshape(n_b, B_HEADS * HD, LANES)
    o = pl.pallas_call(
        functools.partial(_sb_step_kernel, t_q=t_q, q_pos0=q_pos0), grid=(n_b,),
        in_specs=[
            pl.BlockSpec((None, 512, LANES), lambda b: (b, 0, 0)),
            pl.BlockSpec((None, p_len, 512), lambda b: (b, 0, 0)),
            pl.BlockSpec((None, 512, p_len), lambda b: (b, 0, 0)),
            pl.BlockSpec((t_q, 512), lambda b: (b, 0)), pl.BlockSpec((t_q, 512), lambda b: (b, 0)),
        ],
        out_specs=pl.BlockSpec((None, 512, LANES), lambda b: (b, 0, 0)),
        out_shape=jax.ShapeDtypeStruct((n_b, 512, LANES), jnp.float32),
        compiler_params=_cparams(("parallel",)), name="sb_step",
    )(q_heads, kb, vbT, new_kb, new_vb)
    o = jnp.diagonal(o.reshape(n_b, B_HEADS, HD, B_HEADS, t_q), axis1=1, axis2=3)
    return o.transpose(3, 1, 0, 2).reshape(1, B_HEADS * HD, n_b * t_q).astype(MXU_DTYPE)


def _top_values(s, n):
    rows = s.shape[0]
    rid = lax.broadcasted_iota(jnp.int32, s.shape, 0).astype(jnp.float32)
    vals = []
    for _ in range(n):
        m = jnp.max(s, axis=0, keepdims=True)
        first = jnp.min(jnp.where(s == m, rid, float(rows)), axis=0, keepdims=True)
        vals.append(m)
        s = jnp.where(rid == first, -jnp.inf, s)
    return vals


def _top_values_untied(s, n):
    n_inf = _col_reduce(jnp.where(s == -jnp.inf, 1.0, 0.0), jnp.sum)
    vals = []
    for _ in range(n):
        m = jnp.max(s, axis=0, keepdims=True)
        vals.append(m)
        s = jnp.where(s == m, -jnp.inf, s)
    extra = _col_reduce(jnp.where(s == -jnp.inf, 1.0, 0.0), jnp.sum) - n_inf - float(n)
    return vals, extra


def _expert_score_stats(s1, s2, top_values):
    r1, r2 = top_values(s1, P_TOPK + 1), top_values(s2, P_TOPK + 1)
    a1, a2 = (r1[0], r2[0]) if isinstance(r1, tuple) else (r1, r2)
    rc = top_values(_candidate_sums(a1, a2), P_TOPK + 1)
    top = rc[0] if isinstance(rc, tuple) else rc
    pad = [jnp.full_like(top[0], -jnp.inf)] * (3 * SUBLANES - (P_TOPK + 1))
    top = jnp.concatenate(top + pad, axis=0)
    if isinstance(rc, tuple):
        return top, jnp.maximum(jnp.maximum(r1[1], r2[1]), rc[1])
    return top


def _candidate_sums(a1, a2):
    tm = a1[0].shape[1]
    ninf = jnp.full((1, tm), -jnp.inf, jnp.float32)
    pad = [ninf] * (3 * SUBLANES - (P_TOPK + 1))
    a1m = jnp.concatenate(a1 + pad, axis=0)
    a2m = jnp.concatenate(a2 + pad, axis=0)
    r8 = lax.broadcasted_iota(jnp.int32, (SUBLANES, tm), 0)
    pieces = [a1[0] + a2m, a1[1] + a2m[0:SUBLANES]]
    for i in range(2, SUBLANES):
        pieces.append(jnp.where(r8 < (P_TOPK + 1) // (i + 1), a1[i] + a2m[0:SUBLANES], -jnp.inf))
    pieces.append(a1m[SUBLANES:] + a2[0])
    return jnp.concatenate(pieces, axis=0)


def _out_proj_kernel(x_ref, oa_ref, ob_ref, sga_ref, sgb_ref, wpa_ref, wpb_ref, wo_ref, n2_ref,
                     wpq_ref, k1_ref, k2_ref,
                     x1_ref, h2_ref, e1_ref, e2_ref, gthr_ref, q_scr, *, x_token_major):
    m = sga_ref[...] * _dot(wpa_ref[...], oa_ref[...]) + sgb_ref[...] * _dot(wpb_ref[...], ob_ref[...])
    x = x_ref[...].T if x_token_major else x_ref[...]
    x1 = x + _dot(wo_ref[...], m.astype(MXU_DTYPE))
    x1_ref[...] = x1
    ms = jnp.mean(x1 * x1, axis=0, keepdims=True)
    h2 = (x1 * lax.rsqrt(ms + EPS) * n2_ref[...]).astype(MXU_DTYPE)
    h2_ref[...] = h2
    q_scr[...] = _dot(wpq_ref[...], h2).astype(MXU_DTYPE)

    def scores(hh):
        r0 = pl.multiple_of(hh * 128, 128)
        s1 = _dot(k1_ref[hh], q_scr[pl.ds(r0, 64), :])
        s2 = _dot(k2_ref[hh], q_scr[pl.ds(pl.multiple_of(r0 + 64, 64), 64), :])
        return (s1, s2) + _expert_score_stats(s1, s2, _top_values_untied)

    def finish(hh, s1, s2, top, extra):
        top = lax.cond(jnp.max(jnp.abs(extra)) > 0.0,
                       lambda: _expert_score_stats(s1, s2, _top_values), lambda: top)
        mx = top[0:1]
        z = jnp.zeros_like(mx)
        for i in range(P_TOPK):
            z = z + jnp.exp(top[i:i + 1] - mx)
        rz = 0.5 / z
        e1_ref[hh] = jnp.exp(s1 - jnp.max(s1, axis=0, keepdims=True)) * rz
        e2_ref[hh] = jnp.exp(s2 - jnp.max(s2, axis=0, keepdims=True))
        gthr_ref[pl.ds(hh, 1), :] = jnp.exp(0.5 * (top[P_TOPK - 1:P_TOPK] + top[P_TOPK:P_TOPK + 1]) - mx) * rz

    def head_group(i, carry):
        heads = [HEADS_PER_TRIP * i + j for j in range(HEADS_PER_TRIP)]
        stats = [scores(hh) for hh in heads]
        for hh, st in zip(heads, stats):
            finish(hh, *st)
        return carry

    lax.fori_loop(0, P_HEADS // HEADS_PER_TRIP, head_group, 0)


def _out_proj(x, oaT, obT, sgaT, sgbT, w_paT, w_pbT, w_oT, n2, w_pqT, k1, k2, tm, x_token_major):
    n, d = x.shape if x_token_major else x.shape[::-1]
    nt = n // tm
    tpb = oaT.shape[2] // tm
    f32, bf = jnp.float32, MXU_DTYPE
    tok = lambda i: (0, i)
    tok3 = lambda i: (0, 0, i)
    slab = lambda i: (i // tpb, 0, i % tpb)
    c2 = lambda i: (0, 0)
    c3 = lambda i: (0, 0, 0)
    out_shape = (
        jax.ShapeDtypeStruct((d, n), f32), jax.ShapeDtypeStruct((d, n), bf),
        jax.ShapeDtypeStruct((P_HEADS, N_KEYS, n), f32), jax.ShapeDtypeStruct((P_HEADS, N_KEYS, n), f32),
        jax.ShapeDtypeStruct((P_HEADS, n), f32),
    )
    out_specs = (
        pl.BlockSpec((d, tm), tok), pl.BlockSpec((d, tm), tok),
        pl.BlockSpec((P_HEADS, N_KEYS, tm), tok3), pl.BlockSpec((P_HEADS, N_KEYS, tm), tok3),
        pl.BlockSpec((P_HEADS, tm), tok),
    )
    in_specs = [
        pl.BlockSpec((tm, d), lambda i: (i, 0)) if x_token_major else pl.BlockSpec((d, tm), tok),
        pl.BlockSpec((None, 512, tm), slab), pl.BlockSpec((None, 512, tm), slab),
        pl.BlockSpec((d, tm), tok), pl.BlockSpec((d, tm), tok),
        pl.BlockSpec((d, 512), c2), pl.BlockSpec((d, 512), c2), pl.BlockSpec((d, d), c2),
        pl.BlockSpec((d, 1), c2), pl.BlockSpec((P_HEADS * 128, d), c2),
        pl.BlockSpec((P_HEADS, N_KEYS, P_HALF), c3), pl.BlockSpec((P_HEADS, N_KEYS, P_HALF), c3),
    ]
    return pl.pallas_call(
        functools.partial(_out_proj_kernel, x_token_major=x_token_major),
        grid=(nt,), in_specs=in_specs, out_specs=out_specs, out_shape=out_shape,
        scratch_shapes=[pltpu.VMEM((P_HEADS * 128, tm), bf)],
        compiler_params=_cparams(("parallel",)), name="out_proj",
    )(x, oaT, obT, sgaT, sgbT, w_paT, w_pbT, w_oT, n2, w_pqT, k1, k2)


def _twice_gelu_exact(x):
    return x * (1.0 + lax.erf(x * np.float32(math.sqrt(0.5))))


def _peer_kernel(h2_ref, u_ref, vt_ref, e1_ref, e2_ref, gthr_ref, x1_ref, o_ref, acc_ref, coef_ref, *,
                 out_token_major):
    ec = pl.program_id(1)
    tm = h2_ref.shape[1]

    @pl.when(ec == 0)
    def _():
        acc_ref[...] = jnp.zeros_like(acc_ref)

    a_all = _dot(u_ref[...], h2_ref[...])
    n_groups = TE_PEER // (SUBLANES * N_KEYS)
    for grp in range(n_groups):
        i1_group = pl.ds(pl.multiple_of((ec * n_groups + grp) * SUBLANES, SUBLANES), SUBLANES)
        for c in range(tm // LANES):
            cs = slice(c * LANES, (c + 1) * LANES)
            e1g = [e1_ref[hh, i1_group, cs] for hh in range(P_HEADS)]
            gth = [gthr_ref[hh:hh + 1, cs] for hh in range(P_HEADS)]
            for il in range(SUBLANES):
                gate = jnp.zeros((N_KEYS, LANES), jnp.float32)
                for hh in range(P_HEADS):
                    g = e1g[hh][il:il + 1, :] * e2_ref[hh, :, cs]
                    gate = gate + jnp.where(g >= gth[hh], g, 0.0)
                rows = slice((grp * SUBLANES + il) * N_KEYS, (grp * SUBLANES + il + 1) * N_KEYS)
                coef_ref[rows, cs] = (gate * _twice_gelu_exact(a_all[rows, cs])).astype(coef_ref.dtype)
    acc_ref[...] += _dot(vt_ref[...], coef_ref[...])

    @pl.when(ec == pl.num_programs(1) - 1)
    def _():
        x2 = x1_ref[...] + acc_ref[...]
        o_ref[...] = x2.T if out_token_major else x2


def _peer(h2T, u, vT, e1T, e2T, gthr, x1T, tm, out_token_major):
    d, n = x1T.shape
    n_exp = u.shape[0]
    tok = lambda i, e: (0, i)
    tok3 = lambda i, e: (0, 0, i)
    in_specs = [
        pl.BlockSpec((d, tm), tok),
        pl.BlockSpec((TE_PEER, d), lambda i, e: (e, 0)),
        pl.BlockSpec((d, TE_PEER), lambda i, e: (0, e)),
        pl.BlockSpec((P_HEADS, N_KEYS, tm), tok3), pl.BlockSpec((P_HEADS, N_KEYS, tm), tok3),
        pl.BlockSpec((P_HEADS, tm), tok),
        pl.BlockSpec((d, tm), tok),
    ]
    return pl.pallas_call(
        functools.partial(_peer_kernel, out_token_major=out_token_major), grid=(n // tm, n_exp // TE_PEER),
        in_specs=in_specs,
        out_specs=pl.BlockSpec((tm, d), lambda i, e: (i, 0)) if out_token_major else pl.BlockSpec((d, tm), tok),
        out_shape=jax.ShapeDtypeStruct((n, d) if out_token_major else (d, n), jnp.float32),
        scratch_shapes=[pltpu.VMEM((d, tm), jnp.float32), pltpu.VMEM((TE_PEER, tm), MXU_DTYPE)],
        compiler_params=_cparams(("parallel", "arbitrary")), name="peer_dense",
    )(h2T, u, vT, e1T, e2T, gthr, x1T)


def _pack_w_in(w_in):
    d = w_in.shape[0]
    cols = [w_in[:, 0:1280], w_in[:, 1280:1352], jnp.zeros((d, 56), w_in.dtype), w_in[:, 1352:]]
    return jnp.concatenate(cols, axis=1).T.astype(MXU_DTYPE)


def _rope_tables(pos):
    half = HD // 2
    inv = ROPE_THETA ** (-jnp.arange(half, dtype=jnp.float32) / half)
    ang = pos.astype(jnp.float32)[None, :] * inv[:, None]
    return jnp.cos(ang), jnp.sin(ang)


def _token_tile(n, pref):
    return pref if n % pref == 0 else LANES


def _layer(x, lw, rope, n_b, t_q, q_pos0, past, topk, x_token_major, out_token_major):
    (n1, w_inT, qn, kn, ikn, w_paT, w_pbT, w_oT, n2, w_pqT, k1, k2, u, vT) = lw
    n = n_b * t_q
    tm = _token_tile(n, TM_IN_PROJ)
    n_bo = n_b if past is None else 1
    (qaT, qiT, wiT, qbT, vaT, vbT, sgaT, sgbT, ka, va, ki, kb, vb) = _in_proj(
        x, n1, w_inT, qn, kn, ikn, rope[0], rope[1], tm, n_bo, x_token_major)
    new = (ka, va, ki, kb, vb)
    if past is None:
        s_valid = t_q
        k_a = ka.reshape(n_b, t_q, 128)
        k_i = ki.reshape(n_b, t_q, 64)
        k_b = kb.reshape(n_b, t_q, 512)
        v_aT, v_bT = vaT, vbT
        new_a = None
    else:
        k_a, v_aT, k_i, k_b, v_bT = past
        s_valid = k_a.shape[1] + t_q
        new_a = (ka, va, ki)
    oaT = _dsa(qaT, qiT, wiT, k_a, v_aT, k_i, new_a, n_b=n_b, t_q=t_q, q_pos0=q_pos0, s_valid=s_valid, topk=topk)
    if past is None:
        obT = _sb(qbT, k_b, v_bT, n_b=n_b, t_q=t_q)
    else:
        obT = _sb_step(qbT, k_b, v_bT, kb, vb, n_b=n_b, t_q=t_q, q_pos0=q_pos0)
    x1T, h2T, e1T, e2T, gthr = _out_proj(
        x, oaT, obT, sgaT, sgbT, w_paT, w_pbT, w_oT, n2, w_pqT, k1, k2, _token_tile(n, TM_OUT_PROJ), x_token_major)
    x2 = _peer(h2T, u, vT, e1T, e2T, gthr, x1T, _token_tile(n, TM_PEER), out_token_major)
    return x2, new


def kernel(x_prompt, x_sample, cache_a_k, cache_a_v, cache_idx_k, cache_b_k, cache_b_v, norm1, w_in, q_norm_a, k_norm_a, idx_k_norm, w_pa, w_pb, w_o, norm2, peer_wq, peer_k1, peer_k2, peer_u, peer_v):
    n_bp, t_p, d = x_prompt.shape
    n_bs, t_s, _ = x_sample.shape
    depth = w_in.shape[0]
    past_len = cache_a_k.shape[2]
    assert t_p % LANES == 0 and n_bs * t_s == LANES and past_len % LANES == 0 and t_s % SUBLANES == 0
    assert peer_u.shape[1] % TE_PEER == 0
    topk_p = min(TOPK_MAX, t_p // 4)
    topk_s = min(TOPK_MAX, (past_len + t_s) // 4)
    bf = MXU_DTYPE
    col = lambda g: g.reshape(-1, 1)

    rope_p = _rope_tables(jnp.arange(t_p, dtype=jnp.int32))
    rope_s = _rope_tables(jnp.tile(past_len + jnp.arange(t_s, dtype=jnp.int32), n_bs))

    xp = x_prompt.reshape(n_bp * t_p, d)
    xs = x_sample.reshape(n_bs * t_s, d)
    st_p, st_s = [], []
    for l in range(depth):
        first, last = l == 0, l == depth - 1
        lw = (col(norm1[l]), _pack_w_in(w_in[l]), col(q_norm_a[l]), col(k_norm_a[l]), col(idx_k_norm[l]),
              w_pa[l].T.astype(bf), w_pb[l].T.astype(bf), w_o[l].T.astype(bf), col(norm2[l]),
              peer_wq[l].T.astype(bf), peer_k1[l].astype(bf), peer_k2[l].astype(bf),
              peer_u[l].astype(bf), peer_v[l].T.astype(bf))
        past = (cache_a_k[l].reshape(n_bs, past_len, -1),
                cache_a_v[l].reshape(n_bs, past_len, -1).transpose(0, 2, 1).astype(bf),
                cache_idx_k[l],
                cache_b_k[l].reshape(n_bs, past_len, -1),
                cache_b_v[l].reshape(n_bs, past_len, -1).transpose(0, 2, 1).astype(bf))
        xp, sp = _layer(xp, lw, rope_p, n_bp, t_p, 0, None, topk_p, first, last)
        xs, ss = _layer(xs, lw, rope_s, n_bs, t_s, past_len, past, topk_s, first, last)
        st_p.append(sp)
        st_s.append(ss)

    def stack(sts, j, n_b, t, shape):
        return jnp.stack([s[j] for s in sts]).reshape((depth, n_b, t) + shape)

    outs = [xp.reshape(n_bp, t_p, d), xs.reshape(n_bs, t_s, d)]
    shapes = ((A_KV_HEADS, HD), (A_KV_HEADS, HD), (IDX_DIM,), (B_HEADS, HD), (B_HEADS, HD))
    for sts, n_b, t in ((st_p, n_bp, t_p), (st_s, n_bs, t_s)):
        for j, shp in enumerate(shapes):
            outs.append(stack(sts, j, n_b, t, shp))
    return tuple(outs)
```
